```python
import jax
import jax.numpy as jnp
from jax import lax
import numpy as np

D_MODEL = 1024
BATCH = 2
SEQ = 8192
DEPTH = 4

HEAD_DIM = 64
D_ATTN = D_MODEL // 2
D_RWKV = D_MODEL - D_ATTN
N_ATTN_HEADS = D_ATTN // HEAD_DIM
N_RWKV_HEADS = D_RWKV // HEAD_DIM
D_IN = 3 * D_ATTN + 3 * D_RWKV
DILATED_PATTERNS = ((128, 1), (512, 4), (2048, 16))
ATTN_BLOCK = 128
MEM_LEN = 256
N_XATTN_HEADS = 4
XATTN_HEAD_DIM = D_MODEL // N_XATTN_HEADS
D_FF = -(-8 * D_MODEL // (3 * 256)) * 256
LORA_DECAY = 64
LORA_A = 64
LORA_V = 32
LORA_G = 160
NORM_EPS = 1e-6
GN_EPS = 64e-5
MASK_VALUE = -1e30

kernel_name = 'hymba_rwkv7_dilated_hybrid'


def rms_norm(x, gain):
    xf = x.astype(jnp.float32)
    y = xf * lax.rsqrt(jnp.mean(xf * xf, axis=-1, keepdims=True) + NORM_EPS)
    return (y * gain.astype(jnp.float32)).astype(x.dtype)


def token_shift(t):
    return jnp.pad(t, ((0, 0), (1, 0), (0, 0)))[:, :-1]


def lerp_shift(t, mu):
    return t + (token_shift(t) - t) * mu


def _dilated_branch(q, k, v, n_back, dil):
    B, Sp, H, Dh = q.shape
    L = Sp // dil
    nb = L // ATTN_BLOCK

    def split(t):
        t = t.reshape(B, L, dil, H, Dh).transpose(0, 2, 3, 1, 4)
        return t.reshape(B, dil, H, nb, ATTN_BLOCK, Dh)

    def with_prev(t):
        prev = jnp.pad(t, ((0, 0), (0, 0), (0, 0), (1, 0), (0, 0), (0, 0)))[:, :, :, :-1]
        return jnp.concatenate([prev, t], axis=4)

    qb = split(q)
    kc = with_prev(split(k))
    vc = with_prev(split(v))
    s = jnp.einsum('brhnqd,brhnkd->brhnqk', qb, kc, preferred_element_type=jnp.float32) * (Dh ** -0.5)
    qi = jnp.arange(ATTN_BLOCK)[:, None]
    ki = jnp.arange(2 * ATTN_BLOCK)[None, :]
    dist = qi + ATTN_BLOCK - ki
    key_idx = jnp.arange(nb)[:, None, None] * ATTN_BLOCK + ki[None] - ATTN_BLOCK
    valid = (dist >= 0) & (dist <= n_back) & (key_idx >= 0)
    s = jnp.where(valid, s, MASK_VALUE)
    m = jnp.max(s, axis=-1, keepdims=True)
    p = jnp.exp(s - m)
    l = jnp.sum(p, axis=-1, keepdims=True)
    o = jnp.einsum('brhnqk,brhnkd->brhnqd', p, vc.astype(jnp.float32)) / l
    lse = (m + jnp.log(l))[..., 0]
    o = o.reshape(B, dil, H, L, Dh).transpose(0, 3, 1, 2, 4).reshape(B, Sp, H, Dh)
    lse = lse.reshape(B, dil, H, L).transpose(0, 3, 1, 2).reshape(B, Sp, H)
    return o, lse


def dilated_attention(q, k, v):
    B, S, H, Dh = q.shape
    span = max(d for _, d in DILATED_PATTERNS) * ATTN_BLOCK
    s_pad = -(-S // span) * span
    pad = lambda t: jnp.pad(t, ((0, 0), (0, s_pad - S), (0, 0), (0, 0)))
    qp, kp, vp = pad(q), pad(k), pad(v)
    outs, lses = [], []
    for window, dil in DILATED_PATTERNS:
        o, lse = _dilated_branch(qp, kp, vp, window // dil, dil)
        outs.append(o)
        lses.append(lse)
    wts = jax.nn.softmax(jnp.stack(lses, axis=0), axis=0)
    o = jnp.sum(wts[..., None] * jnp.stack(outs, axis=0), axis=0)
    return o[:, :S].astype(q.dtype)


def wkv7_scan(r, w, k, v, a, b):
    B, S, H, N = r.shape

    def step(state, inp):
        r_t, w_t, k_t, v_t, a_t, b_t = inp
        sa = jnp.einsum('bhij,bhj->bhi', state, a_t)
        state = (state * w_t[:, :, None, :] + sa[..., None] * b_t[:, :, None, :]
                 + v_t[..., None] * k_t[:, :, None, :])
        y = jnp.einsum('bhij,bhj->bhi', state, r_t)
        return state, y

    xs = tuple(jnp.moveaxis(t.astype(jnp.float32), 1, 0) for t in (r, w, k, v, a, b))
    state0 = jnp.zeros((B, H, N, N), jnp.float32)
    _, y = lax.scan(step, state0, xs)
    return jnp.moveaxis(y, 0, 1)


def rwkv7_time_mix(h, r_p, k_p, v_p, mu_rkv, mu_lora, w0, w1, w2, a0, a1, a2, g1, g2,
                   k_k, k_a, r_k, ln_w, ln_b, v_first, v_res):
    B, S, _ = h.shape
    H, N = N_RWKV_HEADS, HEAD_DIM
    heads = lambda t: t.reshape(B, S, H, N)
    r = lerp_shift(r_p, mu_rkv[0])
    k = lerp_shift(k_p, mu_rkv[1])
    v = lerp_shift(v_p, mu_rkv[2])
    xw = lerp_shift(h, mu_lora[0])
    xa = lerp_shift(h, mu_lora[1])
    xg = lerp_shift(h, mu_lora[2])
    w_log = -jax.nn.softplus(-(w0 + jnp.tanh(xw @ w1) @ w2)) - 0.5
    decay = jnp.exp(-jnp.exp(w_log.astype(jnp.float32)))
    if v_res is not None:
        mu_v, v0, v1, v2 = v_res
        xv = lerp_shift(h, mu_v)
        v = v + (v_first - v) * jax.nn.sigmoid(v0 + (xv @ v1) @ v2)
    a = jax.nn.sigmoid(a0 + (xa @ a1) @ a2)
    g = jax.nn.sigmoid(xg @ g1) @ g2
    kk = heads(k * k_k).astype(jnp.float32)
    kk = kk / jnp.maximum(jnp.sqrt(jnp.sum(kk * kk, axis=-1, keepdims=True)), 1e-12)
    k = k * (1 + (a - 1) * k_a)
    y = wkv7_scan(heads(r), heads(decay), heads(k), heads(v), -kk, kk * heads(a).astype(jnp.float32))
    mu = jnp.mean(y, axis=-1, keepdims=True)
    var = jnp.mean((y - mu) ** 2, axis=-1, keepdims=True)
    y = ((y - mu) * lax.rsqrt(var + GN_EPS)).reshape(B, S, D_RWKV) * ln_w + ln_b
    bonus = jnp.sum(heads(r) * heads(k) * r_k, axis=-1, keepdims=True) * heads(v)
    y = (y + bonus.reshape(B, S, D_RWKV)) * g
    return y.astype(h.dtype), v


def cross_attention(h, mem_n, wq, wkv, wo):
    B, S, D = h.shape
    M = mem_n.shape[1]
    q = (h @ wq).reshape(B, S, N_XATTN_HEADS, XATTN_HEAD_DIM)
    km, vm = jnp.split(mem_n @ wkv, 2, axis=-1)
    km = km.reshape(B, M, N_XATTN_HEADS, XATTN_HEAD_DIM)
    vm = vm.reshape(B, M, N_XATTN_HEADS, XATTN_HEAD_DIM)
    s = jnp.einsum('bshd,bmhd->bhsm', q, km, preferred_element_type=jnp.float32) * (XATTN_HEAD_DIM ** -0.5)
    p = jax.nn.softmax(s, axis=-1)
    o = jnp.einsum('bhsm,bmhd->bshd', p.astype(vm.dtype), vm).reshape(B, S, D)
    return o @ wo


def setup_inputs(seed: int = 0) -> dict:
    key = jax.random.key(seed)
    ks = iter(jax.random.split(key, 40))
    nrm = lambda shape, scale: jax.random.normal(next(ks), shape, jnp.float32) * scale
    unif = lambda shape: jax.random.uniform(next(ks), shape, jnp.float32, 0.0, 1.0)
    L, D = DEPTH, D_MODEL
    decay_base = jnp.linspace(-6.5, -1.5, D_RWKV, dtype=jnp.float32)
    return {
        'x': nrm((BATCH, SEQ, D), 1.0),
        'mem': nrm((BATCH, MEM_LEN, D), 1.0),
        'mem_norm': 1.0 + nrm((D,), 0.02),
        'norm_mix': 1.0 + nrm((L, D), 0.02),
        'w_in': nrm((L, D, D_IN), D ** -0.5),
        'attn_out_norm': 1.0 + nrm((L, D_ATTN), 0.02),
        'rwkv_mu_rkv': unif((L, 3, D_RWKV)),
        'rwkv_mu_lora': unif((L, 3, D)),
        'rwkv_mu_v': unif((L - 1, D)),
        'rwkv_w0': decay_base + nrm((L, D_RWKV), 0.1),
        'rwkv_w1': nrm((L, D, LORA_DECAY), D ** -0.5),
        'rwkv_w2': nrm((L, LORA_DECAY, D_RWKV), 0.1 * LORA_DECAY ** -0.5),
        'rwkv_a0': nrm((L, D_RWKV), 0.1),
        'rwkv_a1': nrm((L, D, LORA_A), D ** -0.5),
        'rwkv_a2': nrm((L, LORA_A, D_RWKV), 0.1 * LORA_A ** -0.5),
        'rwkv_v0': 1.0 + nrm((L - 1, D_RWKV), 0.1),
        'rwkv_v1': nrm((L - 1, D, LORA_V), D ** -0.5),
        'rwkv_v2': nrm((L - 1, LORA_V, D_RWKV), 0.1 * LORA_V ** -0.5),
        'rwkv_g1': nrm((L, D, LORA_G), D ** -0.5),
        'rwkv_g2': nrm((L, LORA_G, D_RWKV), LORA_G ** -0.5),
        'rwkv_k_k': 0.85 + nrm((L, D_RWKV), 0.02),
        'rwkv_k_a': 1.0 + nrm((L, D_RWKV), 0.02),
        'rwkv_r_k': nrm((L, N_RWKV_HEADS, HEAD_DIM), 0.1),
        'rwkv_ln_w': 1.0 + nrm((L, D_RWKV), 0.02),
        'rwkv_ln_b': nrm((L, D_RWKV), 0.02),
        'w_out': nrm((L, D, D), D ** -0.5),
        'norm_xattn': 1.0 + nrm((L, D), 0.02),
        'xattn_wq': nrm((L, D, D), D ** -0.5),
        'xattn_wkv': nrm((L, D, 2 * D), D ** -0.5),
        'xattn_wo': nrm((L, D, D), D ** -0.5),
        'norm_ffn': 1.0 + nrm((L, D), 0.02),
        'ffn_w_gate': nrm((L, D, D_FF), D ** -0.5),
        'ffn_w_up': nrm((L, D, D_FF), D ** -0.5),
        'ffn_w_down': nrm((L, D_FF, D), D_FF ** -0.5),
        'norm_final': 1.0 + nrm((D,), 0.02),
    }


def reference(x, mem, mem_norm, norm_mix, w_in, attn_out_norm, rwkv_mu_rkv, rwkv_mu_lora, rwkv_mu_v,
              rwkv_w0, rwkv_w1, rwkv_w2, rwkv_a0, rwkv_a1, rwkv_a2, rwkv_v0, rwkv_v1, rwkv_v2,
              rwkv_g1, rwkv_g2, rwkv_k_k, rwkv_k_a, rwkv_r_k, rwkv_ln_w, rwkv_ln_b, w_out,
              norm_xattn, xattn_wq, xattn_wkv, xattn_wo, norm_ffn, ffn_w_gate, ffn_w_up, ffn_w_down,
              norm_final):
    B, S, D = x.shape
    mem_n = rms_norm(mem, mem_norm)
    split_at = [D_ATTN, 2 * D_ATTN, 3 * D_ATTN, 3 * D_ATTN + D_RWKV, 3 * D_ATTN + 2 * D_RWKV]
    attn_heads = lambda t: t.reshape(B, S, N_ATTN_HEADS, HEAD_DIM)
    v_first = None
    for i in range(DEPTH):
        h = rms_norm(x, norm_mix[i])
        q_a, k_a, v_a, r_p, k_p, v_p = jnp.split(h @ w_in[i], split_at, axis=-1)
        attn = dilated_attention(attn_heads(q_a), attn_heads(k_a), attn_heads(v_a)).reshape(B, S, D_ATTN)
        attn = rms_norm(attn, attn_out_norm[i])
        v_res = None if i == 0 else (rwkv_mu_v[i - 1], rwkv_v0[i - 1], rwkv_v1[i - 1], rwkv_v2[i - 1])
        y_rwkv, v_cur = rwkv7_time_mix(h, r_p, k_p, v_p, rwkv_mu_rkv[i], rwkv_mu_lora[i],
                                       rwkv_w0[i], rwkv_w1[i], rwkv_w2[i], rwkv_a0[i], rwkv_a1[i], rwkv_a2[i],
                                       rwkv_g1[i], rwkv_g2[i], rwkv_k_k[i], rwkv_k_a[i], rwkv_r_k[i],
                                       rwkv_ln_w[i], rwkv_ln_b[i], v_first, v_res)
        if i == 0:
            v_first = v_cur
        x = x + jnp.concatenate([attn, y_rwkv], axis=-1) @ w_out[i]
        h = rms_norm(x, norm_xattn[i])
        x = x + cross_attention(h, mem_n, xattn_wq[i], xattn_wkv[i], xattn_wo[i])
        h = rms_norm(x, norm_ffn[i])
        x = x + (jax.nn.silu(h @ ffn_w_gate[i]) * (h @ ffn_w_up[i])) @ ffn_w_down[i]
    return rms_norm(x, norm_final)
```

```python
import functools

import jax
import jax.numpy as jnp
from jax import lax
from jax.experimental import pallas as pl
from jax.experimental.pallas import tpu as pltpu

F32 = jnp.float32
BF16 = jnp.bfloat16

HEAD_DIM = 64
PAIR = 2 * HEAD_DIM
D_GROUP = 512
N_PAIRS = D_GROUP // PAIR
DILATED_PATTERNS = ((128, 1), (512, 4), (2048, 16))
ATTN_BLOCK = 128
N_XATTN_HEADS = 4
NORM_EPS = 1e-6
GN_EPS = 64e-5
MASK_VALUE = -1e30
WKV_CHUNK = 64

V7X_VMEM_LIMIT_BYTES = 56 * 1024 * 1024
TM_MIX = 512
TM_POST = 512
TM_FFN = 512
TW_WKV = 256
FFN_CHUNKS = 2


def _dot(a, b):
    return lax.dot_general(a, b, (((1,), (0,)), ((), ())), preferred_element_type=F32)


def _dot_nt(a, b):
    return lax.dot_general(a, b, (((1,), (1,)), ((), ())), preferred_element_type=F32)


def _dot_tn(a, b):
    return lax.dot_general(a, b, (((0,), (0,)), ((), ())), preferred_element_type=F32)


def _rms_norm(x, gain):
    return x * lax.rsqrt(jnp.mean(x * x, axis=-1, keepdims=True) + NORM_EPS) * gain


def _softplus(z):
    return jnp.maximum(z, 0.0) + jnp.log1p(jnp.exp(-jnp.abs(z)))


def _params(n_grid_dims):
    return pltpu.CompilerParams(dimension_semantics=("arbitrary",) * n_grid_dims,
                                vmem_limit_bytes=V7X_VMEM_LIMIT_BYTES)


def _const_spec(shape, index):
    return pl.BlockSpec(shape, lambda *_: index, pipeline_mode=pl.Buffered(1))


def _layer_spec(arr, layer):
    return _const_spec((None,) + arr.shape[1:], (layer,) + (0,) * (arr.ndim - 1))


def _mem_kv_kernel(mem_ref, gain_ref, wkv_ref, o_ref):
    mem_n = _rms_norm(mem_ref[...], gain_ref[...]).astype(BF16)
    o_ref[...] = _dot(mem_n, wkv_ref[...]).astype(BF16)


def _mem_kv(mem2d, gain, wkv):
    n_layers, d, d2 = wkv.shape
    rows = mem2d.shape[0]
    return pl.pallas_call(
        _mem_kv_kernel,
        grid=(n_layers,),
        in_specs=[pl.BlockSpec((rows, d), lambda l: (0, 0)),
                  pl.BlockSpec((1, d), lambda l: (0, 0)),
                  pl.BlockSpec((None, d, d2), lambda l: (l, 0, 0))],
        out_specs=pl.BlockSpec((None, rows, d2), lambda l: (l, 0, 0)),
        out_shape=jax.ShapeDtypeStruct((n_layers, rows, d2), BF16),
        compiler_params=_params(1),
        name="mem_kv",
    )(mem2d, gain, wkv)


def _mix_in_kernel(*refs, has_vres, tm, tiles_per_seq):
    (x_ref, gain_ref, w_in_ref, mu_rkv_ref, mu_lora_ref, w0_ref, w1_ref, w2_ref, a0_ref, a1_ref,
     a2_ref, g1_ref, g2_ref, kk_ref, ka_ref) = refs[:15]
    rest = refs[15:]
    if has_vres:
        mu_v_ref, v0_ref, v1_ref, v2_ref, vfirst_ref = rest[:5]
        rest = rest[5:]
    (q_out, k_out, v_out, r_out, lw_out, km_out, vv_out, kku_out, a_out, g_out,
     carry_h, carry_p) = rest

    @pl.when(pl.program_id(0) % tiles_per_seq == 0)
    def _start_of_sequence():
        carry_h[...] = jnp.zeros_like(carry_h)
        carry_p[...] = jnp.zeros_like(carry_p)

    h = _rms_norm(x_ref[...], gain_ref[...])
    proj = _dot(h.astype(BF16), w_in_ref[...])
    q_out[...] = (proj[:, 0:D_GROUP] * HEAD_DIM ** -0.5).astype(BF16)
    k_out[...] = proj[:, D_GROUP:2 * D_GROUP].astype(BF16)
    v_out[...] = proj[:, 2 * D_GROUP:3 * D_GROUP].astype(BF16)
    rkv = proj[:, 3 * D_GROUP:]

    row = lax.broadcasted_iota(jnp.int32, (tm, 1), 0)

    def minus_previous_token(t, carry_ref):
        prev = jnp.where(row == 0, carry_ref[...], pltpu.roll(t, 1, 0))
        carry_ref[...] = t[tm - 1:, :]
        return prev - t

    dh = minus_previous_token(h, carry_h)
    drkv = minus_previous_token(rkv, carry_p)

    mu_rkv = mu_rkv_ref[...]
    r = rkv[:, 0:D_GROUP] + drkv[:, 0:D_GROUP] * mu_rkv[0:1]
    k = rkv[:, D_GROUP:2 * D_GROUP] + drkv[:, D_GROUP:2 * D_GROUP] * mu_rkv[1:2]
    v = rkv[:, 2 * D_GROUP:] + drkv[:, 2 * D_GROUP:] * mu_rkv[2:3]

    mu_lora = mu_lora_ref[...]

    def lora(mu_row, w_first_ref, act, w_second_ref):
        hidden = _dot((h + dh * mu_row).astype(BF16), w_first_ref[...])
        return _dot(act(hidden).astype(BF16), w_second_ref[...])

    w_log = -_softplus(-(w0_ref[...] + lora(mu_lora[0:1], w1_ref, jnp.tanh, w2_ref))) - 0.5
    lw_out[...] = -jnp.exp(w_log)
    a = jax.nn.sigmoid(a0_ref[...] + lora(mu_lora[1:2], a1_ref, lambda t: t, a2_ref))
    g_out[...] = lora(mu_lora[2:3], g1_ref, jax.nn.sigmoid, g2_ref)
    if has_vres:
        gate = jax.nn.sigmoid(v0_ref[...] + lora(mu_v_ref[...], v1_ref, lambda t: t, v2_ref))
        v = v + (vfirst_ref[...] - v) * gate

    r_out[...] = r
    vv_out[...] = v
    a_out[...] = a
    kku_out[...] = k * kk_ref[...]
    km_out[...] = k * (1.0 + (a - 1.0) * ka_ref[...])


def _mix_in(x2d, layer, seq_len, p, v_first):
    n_tok, d = x2d.shape
    tm = TM_MIX
    has_vres = layer > 0
    tok = lambda width: pl.BlockSpec((tm, width), lambda i: (i, 0))
    ins = [x2d, p["norm_mix"], p["w_in"], p["mu_rkv"], p["mu_lora"], p["w0"], p["w1"], p["w2"],
           p["a0"], p["a1"], p["a2"], p["g1"], p["g2"], p["k_k"], p["k_a"]]
    in_specs = [tok(d)] + [_layer_spec(a, layer) for a in ins[1:]]
    if has_vres:
        extra = [p["mu_v"], p["v0"], p["v1"], p["v2"]]
        ins += extra + [v_first]
        in_specs += [_layer_spec(a, layer - 1) for a in extra] + [tok(D_GROUP)]
    out_shape = ([jax.ShapeDtypeStruct((n_tok, D_GROUP), BF16)] * 3
                 + [jax.ShapeDtypeStruct((n_tok, D_GROUP), F32)] * 7)
    return pl.pallas_call(
        functools.partial(_mix_in_kernel, has_vres=has_vres, tm=tm, tiles_per_seq=seq_len // tm),
        grid=(n_tok // tm,),
        in_specs=in_specs,
        out_specs=[tok(D_GROUP)] * 10,
        out_shape=out_shape,
        scratch_shapes=[pltpu.VMEM((1, d), F32), pltpu.VMEM((1, 3 * D_GROUP), F32)],
        compiler_params=_params(1),
        name="mix_in",
    )(*ins)


def _attn_kernel(*refs, first, last, n_back):
    q_ref, kc_ref, kp_ref, vc_ref, vp_ref = refs[:5]
    rest = refs[5:]
    if not first:
        acc_in, st_in = rest[:2]
        rest = rest[2:]
    if last:
        gain_ref, o_ref = rest
    else:
        acc_out, st_out = rest

    blk = ATTN_BLOCK
    qi = lax.broadcasted_iota(jnp.int32, (2 * blk, 2 * blk), 0) % blk
    ki = lax.broadcasted_iota(jnp.int32, (2 * blk, 2 * blk), 1)
    dist = qi + blk - ki
    first_key = jnp.where(pl.program_id(2) > 0, 0, blk)
    valid = (dist >= 0) & (dist <= n_back) & (ki >= first_key)
    even = lax.broadcasted_iota(jnp.int32, (1, PAIR), 1) < HEAD_DIM
    stat_lane = lax.broadcasted_iota(jnp.int32, (1, PAIR), 1)
    ones = jnp.ones((2 * blk, PAIR), BF16)

    stats = jnp.zeros((blk, PAIR), F32)
    outs = []
    for pair in range(N_PAIRS):
        cols = slice(pair * PAIR, (pair + 1) * PAIR)
        q = q_ref[:, cols]
        zero = jnp.zeros_like(q)
        qq = jnp.concatenate([jnp.where(even, q, zero), jnp.where(even, zero, q)], axis=0)
        keys = jnp.concatenate([kp_ref[:, cols], kc_ref[:, cols]], axis=0)
        vals = jnp.concatenate([vp_ref[:, cols], vc_ref[:, cols]], axis=0)
        s = jnp.where(valid, _dot_nt(qq, keys), MASK_VALUE)
        m_new = jnp.max(jnp.maximum(s[:, :blk], s[:, blk:]), axis=-1, keepdims=True)
        if not first:
            m_old = jnp.concatenate([st_in[:, 2 * pair:2 * pair + 1],
                                     st_in[:, 2 * pair + 1:2 * pair + 2]], axis=0)
            l_old = jnp.concatenate([st_in[:, 8 + 2 * pair:9 + 2 * pair],
                                     st_in[:, 9 + 2 * pair:10 + 2 * pair]], axis=0)
            m_new = jnp.maximum(m_new, m_old)
        p = jnp.exp(s - m_new).astype(BF16)
        pv = _dot(p, jnp.concatenate([vals, ones], axis=1))
        num = jnp.where(even, pv[:blk, :PAIR], pv[blk:, :PAIR])
        l_new = pv[:, PAIR:PAIR + 1]
        if not first:
            alpha = jnp.exp(m_old - m_new)
            l_new = alpha * l_old + l_new
            num = jnp.where(even, alpha[:blk], alpha[blk:]) * acc_in[:, cols] + num
        if last:
            l_inv = 1.0 / l_new
            outs.append(num * jnp.where(even, l_inv[:blk], l_inv[blk:]))
        else:
            acc_out[:, cols] = num
            for j, col in enumerate((m_new[:blk], m_new[blk:], l_new[:blk], l_new[blk:])):
                lane = 2 * pair + (j % 2) + 8 * (j // 2)
                stats = stats + jnp.where(stat_lane == lane, col, 0.0)
    if last:
        o_ref[...] = _rms_norm(jnp.concatenate(outs, axis=-1), gain_ref[...]).astype(BF16)
    else:
        st_out[...] = stats


def _attn_pattern(q, k, v, state, gain, layer, *, dil, n_back, first, last):
    bsz, seq, width = q.shape
    sub_len = seq // dil
    view = lambda t: t.reshape(bsz, sub_len, dil * t.shape[-1])
    cur = lambda w: pl.BlockSpec((None, ATTN_BLOCK, w), lambda b, r, n: (b, n, r))
    prev = lambda w: pl.BlockSpec((None, ATTN_BLOCK, w), lambda b, r, n: (b, jnp.maximum(n - 1, 0), r))
    ins = [view(q), view(k), view(k), view(v), view(v)]
    in_specs = [cur(width), cur(width), prev(width), cur(width), prev(width)]
    if not first:
        ins += [view(state[0]), view(state[1])]
        in_specs += [cur(width), cur(PAIR)]
    if last:
        ins.append(gain)
        in_specs.append(pl.BlockSpec((None, 1, width), lambda b, r, n: (layer, 0, 0)))
        out_shape = jax.ShapeDtypeStruct((bsz, sub_len, dil * width), BF16)
        out_specs = cur(width)
    else:
        out_shape = [jax.ShapeDtypeStruct((bsz, sub_len, dil * width), F32),
                     jax.ShapeDtypeStruct((bsz, sub_len, dil * PAIR), F32)]
        out_specs = [cur(width), cur(PAIR)]
    out = pl.pallas_call(
        functools.partial(_attn_kernel, first=first, last=last, n_back=n_back),
        grid=(bsz, dil, sub_len // ATTN_BLOCK),
        in_specs=in_specs,
        out_specs=out_specs,
        out_shape=out_shape,
        compiler_params=_params(3),
        name=f"attn_d{dil}",
    )(*ins)
    if last:
        return out.reshape(bsz, seq, width)
    return out[0].reshape(bsz, seq, width), out[1].reshape(bsz, seq, PAIR)


def _dilated_attention(q, k, v, gain, layer):
    span = max(d for _, d in DILATED_PATTERNS) * ATTN_BLOCK
    assert q.shape[1] % span == 0, "sequence length must be a multiple of the largest dilated span"
    state = None
    for idx, (window, dil) in enumerate(DILATED_PATTERNS):
        n_back = window // dil
        assert n_back <= ATTN_BLOCK, "a query block may only reach into the previous key block"
        state = _attn_pattern(q, k, v, state, gain, layer, dil=dil, n_back=n_back,
                              first=idx == 0, last=idx == len(DILATED_PATTERNS) - 1)
    return state


def _split3(x):
    hi = x.astype(BF16)
    r1 = x - hi.astype(F32)
    mid = r1.astype(BF16)
    lo = (r1 - mid.astype(F32)).astype(BF16)
    return hi, mid, lo


def _wkv_kernel(r_ref, lw_ref, k_ref, v_ref, kku_ref, a_ref, g_ref, rk_ref, lnw_ref, lnb_ref,
                o_ref, st_ref, *, tw, steps_per_seq):
    @pl.when(pl.program_id(0) % steps_per_seq == 0)
    def _start_of_sequence():
        st_ref[...] = jnp.zeros_like(st_ref)

    c = WKV_CHUNK
    even = lax.broadcasted_iota(jnp.int32, (1, PAIR), 1) < HEAD_DIM
    ri = lax.broadcasted_iota(jnp.int32, (PAIR, PAIR), 0)
    ci = lax.broadcasted_iota(jnp.int32, (PAIR, PAIR), 1)
    same_head = (ri // c) == (ci // c)
    strictly_lower = same_head & ((ri % c) > (ci % c))
    lower = same_head & ((ri % c) >= (ci % c))
    cumsum_mat = (lax.broadcasted_iota(jnp.int32, (c, c), 0)
                  >= lax.broadcasted_iota(jnp.int32, (c, c), 1)).astype(BF16)
    eye = (ri == ci).astype(F32)
    merge_masks = []
    s = 1
    while s < c:
        merge_masks.append(((ri // (2 * s)) == (ci // (2 * s)))
                           & ((ri // s) % 2 == 1) & ((ci // s) % 2 == 0))
        s *= 2

    def stack(t):
        return jnp.concatenate([jnp.where(even, t, 0.0), jnp.where(even, 0.0, t)], axis=0)

    def head_sum(t):
        s_even = jnp.sum(jnp.where(even, t, 0.0), axis=-1, keepdims=True)
        s_odd = jnp.sum(jnp.where(even, 0.0, t), axis=-1, keepdims=True)
        return jnp.where(even, s_even, s_odd)

    def bdot(a, b):
        return _dot(a.astype(BF16), b.astype(BF16))

    def chunk_step(ic, carry):
        rows = pl.ds(pl.multiple_of(ic * c, c), c)
        for pair in range(N_PAIRS):
            cols = slice(pair * PAIR, (pair + 1) * PAIR)
            lw = lw_ref[rows, cols]
            cum = sum(_dot(cumsum_mat, piece) for piece in _split3(lw))
            total = cum[c - 1:, :]
            w_incl = jnp.exp(cum)
            w_inv = jnp.exp(-cum)
            w_excl = jnp.exp(cum - lw)
            w_rest = jnp.exp(total - cum)

            kku = kku_ref[rows, cols]
            kk = kku / jnp.maximum(jnp.sqrt(head_sum(kku * kku)), 1e-12)
            b_vec = kk * a_ref[rows, cols]
            k_mod = k_ref[rows, cols]
            r = r_ref[rows, cols]
            v = v_ref[rows, cols]
            r_t = r * w_incl

            la = stack(-kk * w_excl)
            lhs = jnp.concatenate([la, stack(r_t)], axis=0).astype(BF16)
            rhs = jnp.concatenate([stack(b_vec * w_inv), stack(k_mod * w_inv)], axis=0).astype(BF16)
            aa = _dot_nt(lhs, rhs)
            a_ab = jnp.where(strictly_lower, aa[:PAIR, :PAIR], 0.0)
            a_ak = jnp.where(strictly_lower, aa[:PAIR, PAIR:], 0.0)
            m_rb = jnp.where(lower, aa[PAIR:, :PAIR], 0.0)
            m_rk = jnp.where(lower, aa[PAIR:, PAIR:], 0.0)

            v_swapped = pltpu.roll(v, HEAD_DIM, 1)
            lv_swapped = jnp.concatenate([jnp.where(even, 0.0, v_swapped),
                                          jnp.where(even, v_swapped, 0.0)], axis=0)
            x = la + bdot(a_ak, lv_swapped)
            t_inv = eye + jnp.where(merge_masks[0], a_ab, 0.0)
            for mask in merge_masks[1:]:
                t_inv = t_inv + bdot(t_inv, bdot(jnp.where(mask, a_ab, 0.0), t_inv))
            x = bdot(t_inv, x)
            p_mat = jnp.where(even, x[:c], x[c:])
            q_mat = pltpu.roll(jnp.where(even, x[c:], x[:c]), HEAD_DIM, 1)

            state = st_ref[pair]
            ur = _dot_nt(jnp.concatenate([p_mat, r_t], axis=0).astype(BF16), state.astype(BF16))
            u = ur[:c] + q_mat
            y_mix = bdot(jnp.concatenate([m_rb, m_rk], axis=1),
                         jnp.concatenate([stack(u), stack(v)], axis=0))
            y = ur[c:] + y_mix[:c] + y_mix[c:]
            update = _dot_tn(jnp.concatenate([u, v], axis=0).astype(BF16),
                             jnp.concatenate([b_vec * w_rest, k_mod * w_rest], axis=0).astype(BF16))
            st_ref[pair] = state * jnp.exp(total) + jnp.where(same_head, update, 0.0)

            mean = head_sum(y) * (1.0 / HEAD_DIM)
            yc = y - mean
            var = head_sum(yc * yc) * (1.0 / HEAD_DIM)
            y_norm = yc * lax.rsqrt(var + GN_EPS) * lnw_ref[:, cols] + lnb_ref[:, cols]
            bonus = head_sum(r * k_mod * rk_ref[:, cols]) * v
            o_ref[rows, cols] = ((y_norm + bonus) * g_ref[rows, cols]).astype(BF16)
        return carry

    lax.fori_loop(0, tw // c, chunk_step, 0)


def _wkv(r, lw, k_mod, v, kku, a, g, layer, seq_len, p):
    n_tok = r.shape[0]
    tw = TW_WKV
    tok = pl.BlockSpec((tw, D_GROUP), lambda i: (i, 0))
    vec = lambda arr: _layer_spec(arr, layer)
    return pl.pallas_call(
        functools.partial(_wkv_kernel, tw=tw, steps_per_seq=seq_len // tw),
        grid=(n_tok // tw,),
        in_specs=[tok] * 7 + [vec(p["r_k"]), vec(p["ln_w"]), vec(p["ln_b"])],
        out_specs=tok,
        out_shape=jax.ShapeDtypeStruct((n_tok, D_GROUP), BF16),
        scratch_shapes=[pltpu.VMEM((N_PAIRS, PAIR, PAIR), F32)],
        compiler_params=_params(1),
        name="wkv",
    )(r, lw, k_mod, v, kku, a, g, p["r_k"], p["ln_w"], p["ln_b"])


def _post_kernel(x_ref, attn_ref, yr_ref, wout_ref, gain_ref, wq_ref, kv_ref, wo_ref, o_ref):
    d = x_ref.shape[-1]
    dh = d // N_XATTN_HEADS
    mixed = jnp.concatenate([attn_ref[...], yr_ref[...]], axis=-1)
    x = x_ref[...] + _dot(mixed, wout_ref[...])
    h = _rms_norm(x, gain_ref[...]).astype(BF16)
    q = (_dot(h, wq_ref[...]) * dh ** -0.5).astype(BF16)
    heads = []
    for hd in range(N_XATTN_HEADS):
        s = _dot_nt(q[:, hd * dh:(hd + 1) * dh], kv_ref[:, hd * dh:(hd + 1) * dh])
        e = jnp.exp(s - jnp.max(s, axis=-1, keepdims=True))
        o = _dot(e.astype(BF16), kv_ref[:, d + hd * dh:d + (hd + 1) * dh])
        heads.append((o / jnp.sum(e, axis=-1, keepdims=True)).astype(BF16))
    o_ref[...] = x + _dot(jnp.concatenate(heads, axis=-1), wo_ref[...])


def _post(x2d, attn2d, yr, kv, layer, seq_len, p):
    n_tok, d = x2d.shape
    tm = TM_POST
    tiles_per_seq = seq_len // tm
    mem_len = kv.shape[2]
    tok = lambda width: pl.BlockSpec((tm, width), lambda i: (i, 0))
    return pl.pallas_call(
        _post_kernel,
        grid=(n_tok // tm,),
        in_specs=[tok(d), tok(D_GROUP), tok(D_GROUP), _layer_spec(p["w_out"], layer),
                  _layer_spec(p["norm_xattn"], layer), _layer_spec(p["wq"], layer),
                  pl.BlockSpec((None, None, mem_len, 2 * d), lambda i: (layer, i // tiles_per_seq, 0, 0)),
                  _layer_spec(p["wo"], layer)],
        out_specs=tok(d),
        out_shape=jax.ShapeDtypeStruct((n_tok, d), F32),
        compiler_params=_params(1),
        name="post",
    )(x2d, attn2d, yr, p["w_out"], p["norm_xattn"], p["wq"], kv, p["wo"])


def _ffn_kernel(*refs, final):
    x_ref, gain_ref, wg_ref, wu_ref, wd_ref = refs[:5]
    o_ref = refs[-1]
    x = x_ref[...]
    h = _rms_norm(x, gain_ref[...]).astype(BF16)
    d_ff = wg_ref.shape[-1]
    step = d_ff // FFN_CHUNKS
    y = x
    for ic in range(FFN_CHUNKS):
        cols = slice(ic * step, (ic + 1) * step)
        gate = _dot(h, wg_ref[:, cols])
        up = _dot(h, wu_ref[:, cols])
        act = (gate * jax.nn.sigmoid(gate) * up).astype(BF16)
        y = y + _dot(act, wd_ref[cols, :])
    if final:
        y = _rms_norm(y, refs[5][...])
    o_ref[...] = y


def _ffn(x2d, layer, p, final_gain):
    n_tok, d = x2d.shape
    tm = TM_FFN
    final = final_gain is not None
    tok = pl.BlockSpec((tm, d), lambda i: (i, 0))
    ins = [x2d, p["norm_ffn"], p["w_gate"], p["w_up"], p["w_down"]]
    in_specs = [tok] + [_layer_spec(a, layer) for a in ins[1:]]
    if final:
        ins.append(final_gain)
        in_specs.append(_const_spec(final_gain.shape, (0, 0)))
    return pl.pallas_call(
        functools.partial(_ffn_kernel, final=final),
        grid=(n_tok // tm,),
        in_specs=in_specs,
        out_specs=tok,
        out_shape=jax.ShapeDtypeStruct((n_tok, d), F32),
        compiler_params=_params(1),
        name="ffn",
    )(*ins)


def kernel(x, mem, mem_norm, norm_mix, w_in, attn_out_norm, rwkv_mu_rkv, rwkv_mu_lora, rwkv_mu_v, rwkv_w0, rwkv_w1, rwkv_w2, rwkv_a0, rwkv_a1, rwkv_a2, rwkv_v0, rwkv_v1, rwkv_v2, rwkv_g1, rwkv_g2, rwkv_k_k, rwkv_k_a, rwkv_r_k, rwkv_ln_w, rwkv_ln_b, w_out, norm_xattn, xattn_wq, xattn_wkv, xattn_wo, norm_ffn, ffn_w_gate, ffn_w_up, ffn_w_down, norm_final):
    bsz, seq, d = x.shape
    n_layers = w_in.shape[0]
    assert d == 2 * D_GROUP and seq % max(TM_MIX, TM_POST, TM_FFN, TW_WKV) == 0
    row = lambda t: t.reshape(t.shape[0], 1, -1)
    p = {
        "norm_mix": row(norm_mix), "w_in": w_in.astype(BF16),
        "mu_rkv": rwkv_mu_rkv, "mu_lora": rwkv_mu_lora, "mu_v": row(rwkv_mu_v),
        "w0": row(rwkv_w0), "w1": rwkv_w1.astype(BF16), "w2": rwkv_w2.astype(BF16),
        "a0": row(rwkv_a0), "a1": rwkv_a1.astype(BF16), "a2": rwkv_a2.astype(BF16),
        "v0": row(rwkv_v0), "v1": rwkv_v1.astype(BF16), "v2": rwkv_v2.astype(BF16),
        "g1": rwkv_g1.astype(BF16), "g2": rwkv_g2.astype(BF16),
        "k_k": row(rwkv_k_k), "k_a": row(rwkv_k_a), "r_k": row(rwkv_r_k),
        "ln_w": row(rwkv_ln_w), "ln_b": row(rwkv_ln_b),
        "w_out": w_out.astype(BF16), "norm_xattn": row(norm_xattn),
        "wq": xattn_wq.astype(BF16), "wo": xattn_wo.astype(BF16),
        "norm_ffn": row(norm_ffn), "w_gate": ffn_w_gate.astype(BF16),
        "w_up": ffn_w_up.astype(BF16), "w_down": ffn_w_down.astype(BF16),
    }
    attn_gain = row(attn_out_norm)
    mem_len = mem.shape[1]
    kv = _mem_kv(mem.reshape(bsz * mem_len, d), mem_norm.reshape(1, d), xattn_wkv.astype(BF16))
    kv = kv.reshape(n_layers, bsz, mem_len, 2 * d)

    x2d = x.reshape(bsz * seq, d)
    v_first = None
    for layer in range(n_layers):
        q, k, v, r, lw, k_mod, vv, kku, a, g = _mix_in(x2d, layer, seq, p, v_first)
        if layer == 0:
            v_first = vv
        as_seq = lambda t: t.reshape(bsz, seq, D_GROUP)
        attn = _dilated_attention(as_seq(q), as_seq(k), as_seq(v), attn_gain, layer)
        yr = _wkv(r, lw, k_mod, vv, kku, a, g, layer, seq, p)
        x2d = _post(x2d, attn.reshape(bsz * seq, D_GROUP), yr, kv, layer, seq, p)
        x2d = _ffn(x2d, layer, p, norm_final.reshape(1, d) if layer == n_layers - 1 else None)
    return x2d.reshape(bsz, seq, d)
```

```python
import functools

import jax
import jax.numpy as jnp
from jax import lax
from jax.experimental import pallas as pl
from jax.experimental.pallas import tpu as pltpu

F32 = jnp.float32
BF16 = jnp.bfloat16

HEAD_DIM = 64
PAIR = 2 * HEAD_DIM
D_GROUP = 512
N_PAIRS = D_GROUP // PAIR
DILATED_PATTERNS = ((128, 1), (512, 4), (2048, 16))
ATTN_BLOCK = 128
N_XATTN_HEADS = 4
NORM_EPS = 1e-6
GN_EPS = 64e-5
MASK_VALUE = -1e30
WKV_CHUNK = 64

V7X_VMEM_LIMIT_BYTES = 56 * 1024 * 1024
TM_MIX = 512
TM_POST = 512
TM_FFN = 512
TW_WKV = 256
FFN_CHUNKS = 2


def _dot(a, b):
    return lax.dot_general(a, b, (((1,), (0,)), ((), ())), preferred_element_type=F32)


def _dot_nt(a, b):
    return lax.dot_general(a, b, (((1,), (1,)), ((), ())), preferred_element_type=F32)


def _dot_tn(a, b):
    return lax.dot_general(a, b, (((0,), (0,)), ((), ())), preferred_element_type=F32)


def _rms_norm(x, gain):
    return x * lax.rsqrt(jnp.mean(x * x, axis=-1, keepdims=True) + NORM_EPS) * gain


def _softplus(z):
    return jnp.maximum(z, 0.0) + jnp.log1p(jnp.exp(-jnp.abs(z)))


def _params(n_grid_dims):
    return pltpu.CompilerParams(dimension_semantics=("arbitrary",) * n_grid_dims,
                                vmem_limit_bytes=V7X_VMEM_LIMIT_BYTES)


def _const_spec(shape, index):
    return pl.BlockSpec(shape, lambda *_: index, pipeline_mode=pl.Buffered(1))


def _layer_spec(arr, layer):
    return _const_spec((None,) + arr.shape[1:], (layer,) + (0,) * (arr.ndim - 1))


def _mem_kv_kernel(mem_ref, gain_ref, wkv_ref, o_ref):
    mem_n = _rms_norm(mem_ref[...], gain_ref[...]).astype(BF16)
    o_ref[...] = _dot(mem_n, wkv_ref[...]).astype(BF16)


def _mem_kv(mem2d, gain, wkv):
    n_layers, d, d2 = wkv.shape
    rows = mem2d.shape[0]
    return pl.pallas_call(
        _mem_kv_kernel,
        grid=(n_layers,),
        in_specs=[pl.BlockSpec((rows, d), lambda l: (0, 0)),
                  pl.BlockSpec((1, d), lambda l: (0, 0)),
                  pl.BlockSpec((None, d, d2), lambda l: (l, 0, 0))],
        out_specs=pl.BlockSpec((None, rows, d2), lambda l: (l, 0, 0)),
        out_shape=jax.ShapeDtypeStruct((n_layers, rows, d2), BF16),
        compiler_params=_params(1),
        name="mem_kv",
    )(mem2d, gain, wkv)


def _mix_in_kernel(*refs, has_vres, tm, tiles_per_seq):
    (x_ref, gain_ref, w_in_ref, mu_rkv_ref, mu_lora_ref, w0_ref, w1_ref, w2_ref, a0_ref, a1_ref,
     a2_ref, g1_ref, g2_ref, kk_ref, ka_ref) = refs[:15]
    rest = refs[15:]
    if has_vres:
        mu_v_ref, v0_ref, v1_ref, v2_ref, vfirst_ref = rest[:5]
        rest = rest[5:]
    (q_out, k_out, v_out, r_out, lw_out, km_out, vv_out, kku_out, a_out, g_out,
     carry_h, carry_p) = rest

    @pl.when(pl.program_id(0) % tiles_per_seq == 0)
    def _start_of_sequence():
        carry_h[...] = jnp.zeros_like(carry_h)
        carry_p[...] = jnp.zeros_like(carry_p)

    h = _rms_norm(x_ref[...], gain_ref[...])
    proj = _dot(h.astype(BF16), w_in_ref[...])
    q_out[...] = (proj[:, 0:D_GROUP] * HEAD_DIM ** -0.5).astype(BF16)
    k_out[...] = proj[:, D_GROUP:2 * D_GROUP].astype(BF16)
    v_out[...] = proj[:, 2 * D_GROUP:3 * D_GROUP].astype(BF16)
    rkv = proj[:, 3 * D_GROUP:]

    row = lax.broadcasted_iota(jnp.int32, (tm, 1), 0)

    def minus_previous_token(t, carry_ref):
        prev = jnp.where(row == 0, carry_ref[...], pltpu.roll(t, 1, 0))
        carry_ref[...] = t[tm - 1:, :]
        return prev - t

    dh = minus_previous_token(h, carry_h)
    drkv = minus_previous_token(rkv, carry_p)

    mu_rkv = mu_rkv_ref[...]
    r = rkv[:, 0:D_GROUP] + drkv[:, 0:D_GROUP] * mu_rkv[0:1]
    k = rkv[:, D_GROUP:2 * D_GROUP] + drkv[:, D_GROUP:2 * D_GROUP] * mu_rkv[1:2]
    v = rkv[:, 2 * D_GROUP:] + drkv[:, 2 * D_GROUP:] * mu_rkv[2:3]

    mu_lora = mu_lora_ref[...]

    def lora(mu_row, w_first_ref, act, w_second_ref):
        hidden = _dot((h + dh * mu_row).astype(BF16), w_first_ref[...])
        return _dot(act(hidden).astype(BF16), w_second_ref[...])

    w_log = -_softplus(-(w0_ref[...] + lora(mu_lora[0:1], w1_ref, jnp.tanh, w2_ref))) - 0.5
    lw_out[...] = -jnp.exp(w_log)
    a = jax.nn.sigmoid(a0_ref[...] + lora(mu_lora[1:2], a1_ref, lambda t: t, a2_ref))
    g_out[...] = lora(mu_lora[2:3], g1_ref, jax.nn.sigmoid, g2_ref)
    if has_vres:
        gate = jax.nn.sigmoid(v0_ref[...] + lora(mu_v_ref[...], v1_ref, lambda t: t, v2_ref))
        v = v + (vfirst_ref[...] - v) * gate

    r_out[...] = r
    vv_out[...] = v
    a_out[...] = a
    kku_out[...] = k * kk_ref[...]
    km_out[...] = k * (1.0 + (a - 1.0) * ka_ref[...])


def _mix_in(x2d, layer, seq_len, p, v_first):
    n_tok, d = x2d.shape
    tm = TM_MIX
    has_vres = layer > 0
    tok = lambda width: pl.BlockSpec((tm, width), lambda i: (i, 0))
    ins = [x2d, p["norm_mix"], p["w_in"], p["mu_rkv"], p["mu_lora"], p["w0"], p["w1"], p["w2"],
           p["a0"], p["a1"], p["a2"], p["g1"], p["g2"], p["k_k"], p["k_a"]]
    in_specs = [tok(d)] + [_layer_spec(a, layer) for a in ins[1:]]
    if has_vres:
        extra = [p["mu_v"], p["v0"], p["v1"], p["v2"]]
        ins += extra + [v_first]
        in_specs += [_layer_spec(a, layer - 1) for a in extra] + [tok(D_GROUP)]
    out_shape = ([jax.ShapeDtypeStruct((n_tok, D_GROUP), BF16)] * 3
                 + [jax.ShapeDtypeStruct((n_tok, D_GROUP), F32)] * 7)
    return pl.pallas_call(
        functools.partial(_mix_in_kernel, has_vres=has_vres, tm=tm, tiles_per_seq=seq_len // tm),
        grid=(n_tok // tm,),
        in_specs=in_specs,
        out_specs=[tok(D_GROUP)] * 10,
        out_shape=out_shape,
        scratch_shapes=[pltpu.VMEM((1, d), F32), pltpu.VMEM((1, 3 * D_GROUP), F32)],
        compiler_params=_params(1),
        name="mix_in",
    )(*ins)


def _attn_kernel(*refs, first, last, n_back):
    q_ref, kc_ref, kp_ref, vc_ref, vp_ref = refs[:5]
    rest = refs[5:]
    if not first:
        acc_in, st_in = rest[:2]
        rest = rest[2:]
    if last:
        gain_ref, o_ref = rest
    else:
        acc_out, st_out = rest

    blk = ATTN_BLOCK
    qi = lax.broadcasted_iota(jnp.int32, (2 * blk, 2 * blk), 0) % blk
    ki = lax.broadcasted_iota(jnp.int32, (2 * blk, 2 * blk), 1)
    dist = qi + blk - ki
    first_key = jnp.where(pl.program_id(2) > 0, 0, blk)
    valid = (dist >= 0) & (dist <= n_back) & (ki >= first_key)
    even = lax.broadcasted_iota(jnp.int32, (1, PAIR), 1) < HEAD_DIM
    stat_lane = lax.broadcasted_iota(jnp.int32, (1, PAIR), 1)
    ones = jnp.ones((2 * blk, PAIR), BF16)

    stats = jnp.zeros((blk, PAIR), F32)
    outs = []
    for pair in range(N_PAIRS):
        cols = slice(pair * PAIR, (pair + 1) * PAIR)
        q = q_ref[:, cols]
        zero = jnp.zeros_like(q)
        qq = jnp.concatenate([jnp.where(even, q, zero), jnp.where(even, zero, q)], axis=0)
        keys = jnp.concatenate([kp_ref[:, cols], kc_ref[:, cols]], axis=0)
        vals = jnp.concatenate([vp_ref[:, cols], vc_ref[:, cols]], axis=0)
        s = jnp.where(valid, _dot_nt(qq, keys), MASK_VALUE)
        m_new = jnp.max(jnp.maximum(s[:, :blk], s[:, blk:]), axis=-1, keepdims=True)
        if not first:
            m_old = jnp.concatenate([st_in[:, 2 * pair:2 * pair + 1],
                                     st_in[:, 2 * pair + 1:2 * pair + 2]], axis=0)
            l_old = jnp.concatenate([st_in[:, 8 + 2 * pair:9 + 2 * pair],
                                     st_in[:, 9 + 2 * pair:10 + 2 * pair]], axis=0)
            m_new = jnp.maximum(m_new, m_old)
        p = jnp.exp(s - m_new).astype(BF16)
        pv = _dot(p, jnp.concatenate([vals, ones], axis=1))
        num = jnp.where(even, pv[:blk, :PAIR], pv[blk:, :PAIR])
        l_new = pv[:, PAIR:PAIR + 1]
        if not first:
            alpha = jnp.exp(m_old - m_new)
            l_new = alpha * l_old + l_new
            num = jnp.where(even, alpha[:blk], alpha[blk:]) * acc_in[:, cols] + num
        if last:
            l_inv = 1.0 / l_new
            outs.append(num * jnp.where(even, l_inv[:blk], l_inv[blk:]))
        else:
            acc_out[:, cols] = num
            for j, col in enumerate((m_new[:blk], m_new[blk:], l_new[:blk], l_new[blk:])):
                lane = 2 * pair + (j % 2) + 8 * (j // 2)
                stats = stats + jnp.where(stat_lane == lane, col, 0.0)
    if last:
        o_ref[...] = _rms_norm(jnp.concatenate(outs, axis=-1), gain_ref[...]).astype(BF16)
    else:
        st_out[...] = stats


def _attn_pattern(q, k, v, state, gain, layer, *, dil, n_back, first, last):
    bsz, seq, width = q.shape
    sub_len = seq // dil
    view = lambda t: t.reshape(bsz, sub_len, dil * t.shape[-1])
    cur = lambda w: pl.BlockSpec((None, ATTN_BLOCK, w), lambda b, r, n: (b, n, r))
    prev = lambda w: pl.BlockSpec((None, ATTN_BLOCK, w), lambda b, r, n: (b, jnp.maximum(n - 1, 0), r))
    ins = [view(q), view(k), view(k), view(v), view(v)]
    in_specs = [cur(width), cur(width), prev(width), cur(width), prev(width)]
    if not first:
        ins += [view(state[0]), view(state[1])]
        in_specs += [cur(width), cur(PAIR)]
    if last:
        ins.append(gain)
        in_specs.append(pl.BlockSpec((None, 1, width), lambda b, r, n: (layer, 0, 0)))
        out_shape = jax.ShapeDtypeStruct((bsz, sub_len, dil * width), BF16)
        out_specs = cur(width)
    else:
        out_shape = [jax.ShapeDtypeStruct((bsz, sub_len, dil * width), F32),
                     jax.ShapeDtypeStruct((bsz, sub_len, dil * PAIR), F32)]
        out_specs = [cur(width), cur(PAIR)]
    out = pl.pallas_call(
        functools.partial(_attn_kernel, first=first, last=last, n_back=n_back),
        grid=(bsz, dil, sub_len // ATTN_BLOCK),
        in_specs=in_specs,
        out_specs=out_specs,
        out_shape=out_shape,
        compiler_params=_params(3),
        name=f"attn_d{dil}",
    )(*ins)
    if last:
        return out.reshape(bsz, seq, width)
    return out[0].reshape(bsz, seq, width), out[1].reshape(bsz, seq, PAIR)


def _dilated_attention(q, k, v, gain, layer):
    span = max(d for _, d in DILATED_PATTERNS) * ATTN_BLOCK
    assert q.shape[1] % span == 0, "sequence length must be a multiple of the largest dilated span"
    state = None
    for idx, (window, dil) in enumerate(DILATED_PATTERNS):
        n_back = window // dil
        assert n_back <= ATTN_BLOCK, "a query block may only reach into the previous key block"
        state = _attn_pattern(q, k, v, state, gain, layer, dil=dil, n_back=n_back,
                              first=idx == 0, last=idx == len(DILATED_PATTERNS) - 1)
    return state


def _split3(x):
    hi = x.astype(BF16)
    r1 = x - hi.astype(F32)
    mid = r1.astype(BF16)
    lo = (r1 - mid.astype(F32)).astype(BF16)
    return hi, mid, lo


def _wkv_kernel(r_ref, lw_ref, k_ref, v_ref, kku_ref, a_ref, g_ref, rk_ref, lnw_ref, lnb_ref,
                o_ref, st_ref, *, tw, steps_per_seq):
    @pl.when(pl.program_id(0) % steps_per_seq == 0)
    def _start_of_sequence():
        st_ref[...] = jnp.zeros_like(st_ref)

    c = WKV_CHUNK
    even = lax.broadcasted_iota(jnp.int32, (1, PAIR), 1) < HEAD_DIM
    ri = lax.broadcasted_iota(jnp.int32, (PAIR, PAIR), 0)
    ci = lax.broadcasted_iota(jnp.int32, (PAIR, PAIR), 1)
    same_head = (ri // c) == (ci // c)
    strictly_lower = same_head & ((ri % c) > (ci % c))
    lower = same_head & ((ri % c) >= (ci % c))
    eye =(ri == ci).astype(F32)
    merge_masks = []
    s = 1
    while s < c:
        merge_masks.append(((ri // (2 * s)) == (ci // (2 * s)))
                           & ((ri // s) % 2 == 1) & ((ci // s) % 2 == 0))
        s *= 2

    def stack(t):
        return jnp.concatenate([jnp.where(even, t, 0.0), jnp.where(even, 0.0, t)], axis=0)

    def head_sum(t):
        s_even = jnp.sum(jnp.where(even, t, 0.0), axis=-1, keepdims=True)
        s_odd = jnp.sum(jnp.where(even, 0.0, t), axis=-1, keepdims=True)
        return jnp.where(even, s_even, s_odd)

    def bdot(a, b):
        return _dot(a.astype(BF16), b.astype(BF16))

    n_chunks = tw // c
    items = [(ic, pair) for ic in range(n_chunks) for pair in range(N_PAIRS)]

    def tile(ref_or_val, ic, pair):
        return ref_or_val[ic * c:(ic + 1) * c, pair * PAIR:(pair + 1) * PAIR]

    lw_all = lw_ref[...]
    bi = lax.broadcasted_iota(jnp.int32, (tw, tw), 0)
    bj = lax.broadcasted_iota(jnp.int32, (tw, tw), 1)
    cumsum_mat = (((bi // c) == (bj // c)) & (bi >= bj)).astype(BF16)
    cum_all = sum(_dot(cumsum_mat, piece) for piece in _split3(lw_all))

    prep = []
    for ic, pair in items:
        lw, cum = tile(lw_all, ic, pair), tile(cum_all, ic, pair)
        total = cum[c - 1:, :]
        kku = tile(kku_ref, ic, pair)
        kk = kku / jnp.maximum(jnp.sqrt(head_sum(kku * kku)), 1e-12)
        b_vec = kk * tile(a_ref, ic, pair)
        k_mod = tile(k_ref, ic, pair)
        r = tile(r_ref, ic, pair)
        v = tile(v_ref, ic, pair)
        w_inv = jnp.exp(-cum)
        w_rest = jnp.exp(total - cum)
        r_t = r * jnp.exp(cum)
        la = stack(-kk * jnp.exp(cum - lw))
        v_swapped = pltpu.roll(v, HEAD_DIM, 1)
        prep.append(dict(
            r=r, v=v, k_mod=k_mod, r_t=r_t, la=la, decay=jnp.exp(total),
            lhs=jnp.concatenate([la, stack(r_t)], axis=0).astype(BF16),
            rhs=jnp.concatenate([stack(b_vec * w_inv), stack(k_mod * w_inv)], axis=0).astype(BF16),
            lv_swapped=jnp.concatenate([jnp.where(even, 0.0, v_swapped),
                                        jnp.where(even, v_swapped, 0.0)], axis=0).astype(BF16),
            bk_rest=jnp.concatenate([b_vec * w_rest, k_mod * w_rest], axis=0).astype(BF16)))

    aas = [_dot_nt(it["lhs"], it["rhs"]) for it in prep]
    a_abs = [jnp.where(strictly_lower, aa[:PAIR, :PAIR], 0.0) for aa in aas]
    xs = [it["la"] + _dot(jnp.where(strictly_lower, aa[:PAIR, PAIR:], 0.0).astype(BF16), it["lv_swapped"])
          for it, aa in zip(prep, aas)]
    m_mix = [jnp.concatenate([jnp.where(lower, aa[PAIR:, :PAIR], 0.0),
                              jnp.where(lower, aa[PAIR:, PAIR:], 0.0)], axis=1).astype(BF16) for aa in aas]
    t_invs = [eye + jnp.where(merge_masks[0], a_ab, 0.0) for a_ab in a_abs]
    for mask in merge_masks[1:]:
        t_bf = [t.astype(BF16) for t in t_invs]
        inner = [_dot(jnp.where(mask, a_ab, 0.0).astype(BF16), t) for a_ab, t in zip(a_abs, t_bf)]
        t_invs = [t + _dot(tb, w.astype(BF16)) for t, tb, w in zip(t_invs, t_bf, inner)]
    xs = [bdot(t, x) for t, x in zip(t_invs, xs)]

    out_rows = []
    states = [st_ref[pair] for pair in range(N_PAIRS)]
    for ic in range(n_chunks):
        chunk = [(prep[ic * N_PAIRS + pair], xs[ic * N_PAIRS + pair], m_mix[ic * N_PAIRS + pair])
                 for pair in range(N_PAIRS)]
        p_mats = [jnp.where(even, x[:c], x[c:]) for _, x, _ in chunk]
        q_mats = [pltpu.roll(jnp.where(even, x[c:], x[:c]), HEAD_DIM, 1) for _, x, _ in chunk]
        urs = [_dot_nt(jnp.concatenate([p_mat, it["r_t"]], axis=0).astype(BF16), state.astype(BF16))
               for (it, _, _), p_mat, state in zip(chunk, p_mats, states)]
        us = [ur[:c] + q_mat for ur, q_mat in zip(urs, q_mats)]
        y_mixes = [_dot(mm, jnp.concatenate([stack(u), stack(it["v"])], axis=0).astype(BF16))
                   for (it, _, mm), u in zip(chunk, us)]
        updates = [_dot_tn(jnp.concatenate([u, it["v"]], axis=0).astype(BF16), it["bk_rest"])
                   for (it, _, _), u in zip(chunk, us)]
        states = [state * it["decay"] + jnp.where(same_head, update, 0.0)
                  for (it, _, _), state, update in zip(chunk, states, updates)]
        outs = []
        for pair, ((it, _, _), ur, y_mix) in enumerate(zip(chunk, urs, y_mixes)):
            cols = slice(pair * PAIR, (pair + 1) * PAIR)
            y = ur[c:] + y_mix[:c] + y_mix[c:]
            mean = head_sum(y) * (1.0 / HEAD_DIM)
            yc = y - mean
            var = head_sum(yc * yc) * (1.0 / HEAD_DIM)
            y_norm = yc * lax.rsqrt(var + GN_EPS) * lnw_ref[:, cols] + lnb_ref[:, cols]
            bonus = head_sum(it["r"] * it["k_mod"] * rk_ref[:, cols]) * it["v"]
            outs.append(((y_norm + bonus) * tile(g_ref, ic, pair)).astype(BF16))
        out_rows.append(jnp.concatenate(outs, axis=-1))
    for pair in range(N_PAIRS):
        st_ref[pair] = states[pair]
    o_ref[...] = jnp.concatenate(out_rows, axis=0)


def _wkv(r, lw, k_mod, v, kku, a, g, layer, seq_len, p):
    n_tok = r.shape[0]
    tw = TW_WKV
    tok = pl.BlockSpec((tw, D_GROUP), lambda i: (i, 0))
    vec = lambda arr: _layer_spec(arr, layer)
    return pl.pallas_call(
        functools.partial(_wkv_kernel, tw=tw, steps_per_seq=seq_len // tw),
        grid=(n_tok // tw,),
        in_specs=[tok] * 7 + [vec(p["r_k"]), vec(p["ln_w"]), vec(p["ln_b"])],
        out_specs=tok,
        out_shape=jax.ShapeDtypeStruct((n_tok, D_GROUP), BF16),
        scratch_shapes=[pltpu.VMEM((N_PAIRS, PAIR, PAIR), F32)],
        compiler_params=_params(1),
        name="wkv",
    )(r, lw, k_mod, v, kku, a, g, p["r_k"], p["ln_w"], p["ln_b"])


def _post_kernel(x_ref, attn_ref, yr_ref, wout_ref, gain_ref, wq_ref, kv_ref, wo_ref, o_ref):
    d = x_ref.shape[-1]
    dh = d // N_XATTN_HEADS
    mixed = jnp.concatenate([attn_ref[...], yr_ref[...]], axis=-1)
    x = x_ref[...] + _dot(mixed, wout_ref[...])
    h = _rms_norm(x, gain_ref[...]).astype(BF16)
    q = (_dot(h, wq_ref[...]) * dh ** -0.5).astype(BF16)
    heads = []
    for hd in range(N_XATTN_HEADS):
        s = _dot_nt(q[:, hd * dh:(hd + 1) * dh], kv_ref[:, hd * dh:(hd + 1) * dh])
        e = jnp.exp(s - jnp.max(s, axis=-1, keepdims=True))
        o = _dot(e.astype(BF16), kv_ref[:, d + hd * dh:d + (hd + 1) * dh])
        heads.append((o / jnp.sum(e, axis=-1, keepdims=True)).astype(BF16))
    o_ref[...] = x + _dot(jnp.concatenate(heads, axis=-1), wo_ref[...])


def _post(x2d, attn2d, yr, kv, layer, seq_len, p):
    n_tok, d = x2d.shape
    tm = TM_POST
    tiles_per_seq = seq_len // tm
    mem_len = kv.shape[2]
    tok = lambda width: pl.BlockSpec((tm, width), lambda i: (i, 0))
    return pl.pallas_call(
        _post_kernel,
        grid=(n_tok // tm,),
        in_specs=[tok(d), tok(D_GROUP), tok(D_GROUP), _layer_spec(p["w_out"], layer),
                  _layer_spec(p["norm_xattn"], layer), _layer_spec(p["wq"], layer),
                  pl.BlockSpec((None, None, mem_len, 2 * d), lambda i: (layer, i // tiles_per_seq, 0, 0)),
                  _layer_spec(p["wo"], layer)],
        out_specs=tok(d),
        out_shape=jax.ShapeDtypeStruct((n_tok, d), F32),
        compiler_params=_params(1),
        name="post",
    )(x2d, attn2d, yr, p["w_out"], p["norm_xattn"], p["wq"], kv, p["wo"])


def _ffn_kernel(*refs, final):
    x_ref, gain_ref, wg_ref, wu_ref, wd_ref = refs[:5]
    o_ref = refs[-1]
    x = x_ref[...]
    h = _rms_norm(x, gain_ref[...]).astype(BF16)
    d_ff = wg_ref.shape[-1]
    step = d_ff // FFN_CHUNKS
    y = x
    for ic in range(FFN_CHUNKS):
        cols = slice(ic * step, (ic + 1) * step)
        gate = _dot(h, wg_ref[:, cols])
        up = _dot(h, wu_ref[:, cols])
        act = (gate * jax.nn.sigmoid(gate) * up).astype(BF16)
        y = y + _dot(act, wd_ref[cols, :])
    if final:
        y = _rms_norm(y, refs[5][...])
    o_ref[...] = y


def _ffn(x2d, layer, p, final_gain):
    n_tok, d = x2d.shape
    tm = TM_FFN
    final = final_gain is not None
    tok = pl.BlockSpec((tm, d), lambda i: (i, 0))
    ins = [x2d, p["norm_ffn"], p["w_gate"], p["w_up"], p["w_down"]]
    in_specs = [tok] + [_layer_spec(a, layer) for a in ins[1:]]
    if final:
        ins.append(final_gain)
        in_specs.append(_const_spec(final_gain.shape, (0, 0)))
    return pl.pallas_call(
        functools.partial(_ffn_kernel, final=final),
        grid=(n_tok // tm,),
        in_specs=in_specs,
        out_specs=tok,
        out_shape=jax.ShapeDtypeStruct((n_tok, d), F32),
        compiler_params=_params(1),
        name="ffn",
    )(*ins)


def kernel(x, mem, mem_norm, norm_mix, w_in, attn_out_norm, rwkv_mu_rkv, rwkv_mu_lora, rwkv_mu_v, rwkv_w0, rwkv_w1, rwkv_w2, rwkv_a0, rwkv_a1, rwkv_a2, rwkv_v0, rwkv_v1, rwkv_v2, rwkv_g1, rwkv_g2, rwkv_k_k, rwkv_k_a, rwkv_r_k, rwkv_ln_w, rwkv_ln_b, w_out, norm_xattn, xattn_wq, xattn_wkv, xattn_wo, norm_ffn, ffn_w_gate, ffn_w_up, ffn_w_down, norm_final):
    bsz, seq, d = x.shape
    n_layers = w_in.shape[0]
    assert d == 2 * D_GROUP and seq % max(TM_MIX, TM_POST, TM_FFN, TW_WKV) == 0
    row = lambda t: t.reshape(t.shape[0], 1, -1)
    p = {
        "norm_mix": row(norm_mix), "w_in": w_in.astype(BF16),
        "mu_rkv": rwkv_mu_rkv, "mu_lora": rwkv_mu_lora, "mu_v": row(rwkv_mu_v),
        "w0": row(rwkv_w0), "w1": rwkv_w1.astype(BF16), "w2": rwkv_w2.astype(BF16),
        "a0": row(rwkv_a0), "a1": rwkv_a1.astype(BF16), "a2": rwkv_a2.astype(BF16),
        "v0": row(rwkv_v0), "v1": rwkv_v1.astype(BF16), "v2": rwkv_v2.astype(BF16),
        "g1": rwkv_g1.astype(BF16), "g2": rwkv_g2.astype(BF16),
        "k_k": row(rwkv_k_k), "k_a": row(rwkv_k_a), "r_k": row(rwkv_r_k),
        "ln_w": row(rwkv_ln_w), "ln_b": row(rwkv_ln_b),
        "w_out": w_out.astype(BF16), "norm_xattn": row(norm_xattn),
        "wq": xattn_wq.astype(BF16), "wo": xattn_wo.astype(BF16),
        "norm_ffn": row(norm_ffn), "w_gate": ffn_w_gate.astype(BF16),
        "w_up": ffn_w_up.astype(BF16), "w_down": ffn_w_down.astype(BF16),
    }
    attn_gain = row(attn_out_norm)
    mem_len = mem.shape[1]
    kv = _mem_kv(mem.reshape(bsz * mem_len, d), mem_norm.reshape(1, d), xattn_wkv.astype(BF16))
    kv = kv.reshape(n_layers, bsz, mem_len, 2 * d)

    x2d = x.reshape(bsz * seq, d)
    v_first = None
    for layer in range(n_layers):
        q, k, v, r, lw, k_mod, vv, kku, a, g = _mix_in(x2d, layer, seq, p, v_first)
        if layer == 0:
            v_first = vv
        as_seq = lambda t: t.reshape(bsz, seq, D_GROUP)
        attn = _dilated_attention(as_seq(q), as_seq(k), as_seq(v), attn_gain, layer)
        yr = _wkv(r, lw, k_mod, vv, kku, a, g, layer, seq, p)
        x2d = _post(x2d, attn.reshape(bsz * seq, D_GROUP), yr, kv, layer, seq, p)
        x2d = _ffn(x2d, layer, p, norm_final.reshape(1, d) if layer == n_layers - 1 else None)
    return x2d.reshape(bsz, seq, d)
```

```python
import functools

import jax
import jax.numpy as jnp
from jax import lax
from jax.experimental import pallas as pl
from jax.experimental.pallas import tpu as pltpu

F32 = jnp.float32
BF16 = jnp.bfloat16

HEAD_DIM = 64
PAIR = 2 * HEAD_DIM
D_GROUP = 512
N_PAIRS = D_GROUP // PAIR
DILATED_PATTERNS = ((128, 1), (512, 4), (2048, 16))
ATTN_BLOCK = 128
ATTN_SPAN = ATTN_BLOCK * max(d for _, d in DILATED_PATTERNS)
N_XATTN_HEADS = 4
NORM_EPS = 1e-6
GN_EPS = 64e-5
MASK_VALUE = -1e30
WKV_CHUNK = 64

V7X_VMEM_LIMIT_BYTES = 56 * 1024 * 1024
TM_MIX = 512
TM_POST = 512
TM_FFN = 512
TW_WKV = 256
FFN_CHUNKS = 2


def _dot(a, b):
    return lax.dot_general(a, b, (((1,), (0,)), ((), ())), preferred_element_type=F32)


def _dot_nt(a, b):
    return lax.dot_general(a, b, (((1,), (1,)), ((), ())), preferred_element_type=F32)


def _dot_tn(a, b):
    return lax.dot_general(a, b, (((0,), (0,)), ((), ())), preferred_element_type=F32)


def _rms_norm(x, gain):
    return x * lax.rsqrt(jnp.mean(x * x, axis=-1, keepdims=True) + NORM_EPS) * gain


def _softplus(z):
    return jnp.maximum(z, 0.0) + jnp.log1p(jnp.exp(-jnp.abs(z)))


def _params(n_grid_dims):
    return pltpu.CompilerParams(dimension_semantics=("arbitrary",) * n_grid_dims,
                                vmem_limit_bytes=V7X_VMEM_LIMIT_BYTES)


def _const_spec(shape, index):
    return pl.BlockSpec(shape, lambda *_: index, pipeline_mode=pl.Buffered(1))


def _layer_spec(arr, layer):
    return _const_spec((None,) + arr.shape[1:], (layer,) + (0,) * (arr.ndim - 1))


def _mem_kv_kernel(mem_ref, gain_ref, wkv_ref, o_ref):
    mem_n = _rms_norm(mem_ref[...], gain_ref[...]).astype(BF16)
    o_ref[...] = _dot(mem_n, wkv_ref[...]).astype(BF16)


def _mem_kv(mem2d, gain, wkv):
    n_layers, d, d2 = wkv.shape
    rows = mem2d.shape[0]
    return pl.pallas_call(
        _mem_kv_kernel,
        grid=(n_layers,),
        in_specs=[pl.BlockSpec((rows, d), lambda l: (0, 0)),
                  pl.BlockSpec((1, d), lambda l: (0, 0)),
                  pl.BlockSpec((None, d, d2), lambda l: (l, 0, 0))],
        out_specs=pl.BlockSpec((None, rows, d2), lambda l: (l, 0, 0)),
        out_shape=jax.ShapeDtypeStruct((n_layers, rows, d2), BF16),
        compiler_params=_params(1),
        name="mem_kv",
    )(mem2d, gain, wkv)


def _mix_in_kernel(*refs, has_vres, tm, tiles_per_seq):
    (x_ref, gain_ref, w_in_ref, mu_rkv_ref, mu_lora_ref, w0_ref, w1_ref, w2_ref, a0_ref, a1_ref,
     a2_ref, g1_ref, g2_ref, kk_ref, ka_ref) = refs[:15]
    rest = refs[15:]
    if has_vres:
        mu_v_ref, v0_ref, v1_ref, v2_ref, vfirst_ref = rest[:5]
        rest = rest[5:]
    (q_out, k_out, v_out, r_out, lw_out, km_out, vv_out, kku_out, a_out, g_out,
     carry_h, carry_p) = rest

    @pl.when(pl.program_id(0) % tiles_per_seq == 0)
    def _start_of_sequence():
        carry_h[...] = jnp.zeros_like(carry_h)
        carry_p[...] = jnp.zeros_like(carry_p)

    h = _rms_norm(x_ref[...], gain_ref[...])
    proj = _dot(h.astype(BF16), w_in_ref[...])
    q_out[...] = (proj[:, 0:D_GROUP] * HEAD_DIM ** -0.5).astype(BF16)
    k_out[...] = proj[:, D_GROUP:2 * D_GROUP].astype(BF16)
    v_out[...] = proj[:, 2 * D_GROUP:3 * D_GROUP].astype(BF16)
    rkv = proj[:, 3 * D_GROUP:]

    row = lax.broadcasted_iota(jnp.int32, (tm, 1), 0)

    def minus_previous_token(t, carry_ref):
        prev = jnp.where(row == 0, carry_ref[...], pltpu.roll(t, 1, 0))
        carry_ref[...] = t[tm - 1:, :]
        return prev - t

    dh = minus_previous_token(h, carry_h)
    drkv = minus_previous_token(rkv, carry_p)

    mu_rkv = mu_rkv_ref[...]
    r = rkv[:, 0:D_GROUP] + drkv[:, 0:D_GROUP] * mu_rkv[0:1]
    k = rkv[:, D_GROUP:2 * D_GROUP] + drkv[:, D_GROUP:2 * D_GROUP] * mu_rkv[1:2]
    v = rkv[:, 2 * D_GROUP:] + drkv[:, 2 * D_GROUP:] * mu_rkv[2:3]

    mu_lora = mu_lora_ref[...]

    def lora(mu_row, w_first_ref, act, w_second_ref):
        hidden = _dot((h + dh * mu_row).astype(BF16), w_first_ref[...])
        return _dot(act(hidden).astype(BF16), w_second_ref[...])

    w_log = -_softplus(-(w0_ref[...] + lora(mu_lora[0:1], w1_ref, jnp.tanh, w2_ref))) - 0.5
    lw_out[...] = -jnp.exp(w_log)
    a = jax.nn.sigmoid(a0_ref[...] + lora(mu_lora[1:2], a1_ref, lambda t: t, a2_ref))
    g_out[...] = lora(mu_lora[2:3], g1_ref, jax.nn.sigmoid, g2_ref)
    if has_vres:
        gate = jax.nn.sigmoid(v0_ref[...] + lora(mu_v_ref[...], v1_ref, lambda t: t, v2_ref))
        v = v + (vfirst_ref[...] - v) * gate

    r_out[...] = r
    vv_out[...] = v
    a_out[...] = a
    kku_out[...] = k * kk_ref[...]
    km_out[...] = k * (1.0 + (a - 1.0) * ka_ref[...])


def _mix_in(x2d, layer, seq_len, p, v_first):
    n_tok, d = x2d.shape
    tm = TM_MIX
    has_vres = layer > 0
    tok = lambda width: pl.BlockSpec((tm, width), lambda i: (i, 0))
    ins = [x2d, p["norm_mix"], p["w_in"], p["mu_rkv"], p["mu_lora"], p["w0"], p["w1"], p["w2"],
           p["a0"], p["a1"], p["a2"], p["g1"], p["g2"], p["k_k"], p["k_a"]]
    in_specs = [tok(d)] + [_layer_spec(a, layer) for a in ins[1:]]
    if has_vres:
        extra = [p["mu_v"], p["v0"], p["v1"], p["v2"]]
        ins += extra + [v_first]
        in_specs += [_layer_spec(a, layer - 1) for a in extra] + [tok(D_GROUP)]
    out_shape = ([jax.ShapeDtypeStruct((n_tok, D_GROUP), BF16)] * 3
                 + [jax.ShapeDtypeStruct((n_tok, D_GROUP), F32)] * 7)
    return pl.pallas_call(
        functools.partial(_mix_in_kernel, has_vres=has_vres, tm=tm, tiles_per_seq=seq_len // tm),
        grid=(n_tok // tm,),
        in_specs=in_specs,
        out_specs=[tok(D_GROUP)] * 10,
        out_shape=out_shape,
        scratch_shapes=[pltpu.VMEM((1, d), F32), pltpu.VMEM((1, 3 * D_GROUP), F32)],
        compiler_params=_params(1),
        name="mix_in",
    )(*ins)


def _attn_kernel(q_ref, k_ref, v_ref, gain_ref, o_ref, q_st, kv_st, acc_ref, stat_ref, *,
                 spans_per_seq, patterns):
    blk, span = ATTN_BLOCK, ATTN_SPAN
    j = pl.program_id(0) % spans_per_seq
    cur_base = (j % 2) * span
    prev_base = span - cur_base

    @pl.when(j == 0)
    def _no_previous_span():
        kv_st[:, pl.ds(prev_base, span), :] = jnp.zeros((2 * N_PAIRS, span, PAIR), F32)

    for pair in range(N_PAIRS):
        cols = slice(pair * PAIR, (pair + 1) * PAIR)
        q_st[pair] = q_ref[:, cols].astype(F32)
        kv_st[pair, pl.ds(cur_base, span), :] = k_ref[:, cols].astype(F32)
        kv_st[N_PAIRS + pair, pl.ds(cur_base, span), :] = v_ref[:, cols].astype(F32)

    qi = lax.broadcasted_iota(jnp.int32, (2 * blk, 2 * blk), 0) % blk
    ki = lax.broadcasted_iota(jnp.int32, (2 * blk, 2 * blk), 1)
    dist = qi + blk - ki
    even = lax.broadcasted_iota(jnp.int32, (1, PAIR), 1) < HEAD_DIM
    stat_lane = lax.broadcasted_iota(jnp.int32, (1, PAIR), 1)
    ones = jnp.ones((2 * blk, PAIR), BF16)
    pairs = range(N_PAIRS)

    def run_pass(n_back, dil, first, last):
        blocks_per_sub = span // (blk * dil)
        shift = blocks_per_sub.bit_length() - 1
        stride = None if dil == 1 else dil

        def rows(start):
            return pl.ds(start, blk, stride=stride) if stride else pl.ds(pl.multiple_of(start, blk), blk)

        def block_step(t, carry):
            r = lax.shift_right_logical(t, shift)
            n = t & (blocks_per_sub - 1)
            start = n * (blk * dil) + r
            cur = cur_base + start
            prev = jnp.where(n > 0, cur - blk * dil, prev_base + span - blk * dil + r)
            first_key = jnp.where((n > 0) | (j > 0), 0, blk)
            valid = (dist >= 0) & (dist <= n_back) & (ki >= first_key)

            qs = [q_st[p, rows(start), :].astype(BF16) for p in pairs]
            keys = [jnp.concatenate([kv_st[p, rows(prev), :].astype(BF16),
                                     kv_st[p, rows(cur), :].astype(BF16)], axis=0) for p in pairs]
            vals = [jnp.concatenate([kv_st[N_PAIRS + p, rows(prev), :].astype(BF16),
                                     kv_st[N_PAIRS + p, rows(cur), :].astype(BF16)], axis=0) for p in pairs]
            zero = jnp.zeros((blk, PAIR), BF16)
            qqs = [jnp.concatenate([jnp.where(even, q, zero), jnp.where(even, zero, q)], axis=0) for q in qs]
            ss = [jnp.where(valid, _dot_nt(qq, key), MASK_VALUE) for qq, key in zip(qqs, keys)]
            m_news = [jnp.max(jnp.maximum(s[:, :blk], s[:, blk:]), axis=-1, keepdims=True) for s in ss]
            if not first:
                stats_old = stat_ref[rows(start), :]
                m_olds = [jnp.concatenate([stats_old[:, 2 * p:2 * p + 1], stats_old[:, 2 * p + 1:2 * p + 2]], axis=0)
                          for p in pairs]
                l_olds = [jnp.concatenate([stats_old[:, 8 + 2 * p:9 + 2 * p], stats_old[:, 9 + 2 * p:10 + 2 * p]],
                                          axis=0) for p in pairs]
                m_news = [jnp.maximum(m_new, m_old) for m_new, m_old in zip(m_news, m_olds)]
            ps = [jnp.exp(s - m_new).astype(BF16) for s, m_new in zip(ss, m_news)]
            pvs = [_dot(p, jnp.concatenate([val, ones], axis=1)) for p, val in zip(ps, vals)]
            nums = [jnp.where(even, pv[:blk, :PAIR], pv[blk:, :PAIR]) for pv in pvs]
            l_news = [pv[:, PAIR:PAIR + 1] for pv in pvs]
            if not first:
                alphas = [jnp.exp(m_old - m_new) for m_old, m_new in zip(m_olds, m_news)]
                l_news = [alpha * l_old + l_new for alpha, l_old, l_new in zip(alphas, l_olds, l_news)]
                nums = [jnp.where(even, alpha[:blk], alpha[blk:]) * acc_ref[p, rows(start), :] + num
                        for p, alpha, num in zip(pairs, alphas, nums)]
            if last:
                l_invs = [1.0 / l_new for l_new in l_news]
                out = jnp.concatenate([num * jnp.where(even, l_inv[:blk], l_inv[blk:])
                                       for num, l_inv in zip(nums, l_invs)], axis=-1)
                o_ref[rows(start), :] = _rms_norm(out, gain_ref[...]).astype(BF16)
            else:
                stats = jnp.zeros((blk, PAIR), F32)
                for p in pairs:
                    acc_ref[p, rows(start), :] = nums[p]
                    cols4 = (m_news[p][:blk], m_news[p][blk:], l_news[p][:blk], l_news[p][blk:])
                    for idx, col in enumerate(cols4):
                        lane = 2 * p + (idx % 2) + 8 * (idx // 2)
                        stats = stats + jnp.where(stat_lane == lane, col, 0.0)
                stat_ref[rows(start), :] = stats
            return carry

        lax.fori_loop(0, span // blk, block_step, 0)

    for idx, (n_back, dil) in enumerate(patterns):
        run_pass(n_back, dil, first=idx == 0, last=idx == len(patterns) - 1)


def _dilated_attention(q, k, v, gain, layer, seq_len):
    n_tok, width = q.shape
    assert seq_len % ATTN_SPAN == 0, "sequence length must be a multiple of the largest dilated span"
    patterns = tuple(sorted(((w // d, d) for w, d in DILATED_PATTERNS), key=lambda nd: -nd[1]))
    assert patterns[-1][1] == 1 and all(ATTN_SPAN % (ATTN_BLOCK * d) == 0 for _, d in patterns)
    assert all(n_back <= ATTN_BLOCK for n_back, _ in patterns), \
        "a query block may only reach into the previous key block"
    tok = pl.BlockSpec((ATTN_SPAN, width), lambda i: (i, 0))
    return pl.pallas_call(
        functools.partial(_attn_kernel, spans_per_seq=seq_len // ATTN_SPAN, patterns=patterns),
        grid=(n_tok // ATTN_SPAN,),
        in_specs=[tok, tok, tok, _layer_spec(gain, layer)],
        out_specs=tok,
        out_shape=jax.ShapeDtypeStruct((n_tok, width), BF16),
        scratch_shapes=[pltpu.VMEM((N_PAIRS, ATTN_SPAN, PAIR), F32),
                        pltpu.VMEM((2 * N_PAIRS, 2 * ATTN_SPAN, PAIR), F32),
                        pltpu.VMEM((N_PAIRS, ATTN_SPAN, PAIR), F32),
                        pltpu.VMEM((ATTN_SPAN, PAIR), F32)],
        compiler_params=_params(1),
        name="attn",
    )(q, k, v, gain)


def _split3(x):
    hi = x.astype(BF16)
    r1 = x - hi.astype(F32)
    mid = r1.astype(BF16)
    lo = (r1 - mid.astype(F32)).astype(BF16)
    return hi, mid, lo


def _wkv_kernel(r_ref, lw_ref, k_ref, v_ref, kku_ref, a_ref, g_ref, rk_ref, lnw_ref, lnb_ref,
                o_ref, st_ref, *, tw, steps_per_seq):
    @pl.when(pl.program_id(0) % steps_per_seq == 0)
    def _start_of_sequence():
        st_ref[...] = jnp.zeros_like(st_ref)

    c = WKV_CHUNK
    even = lax.broadcasted_iota(jnp.int32, (1, PAIR), 1) < HEAD_DIM
    ri = lax.broadcasted_iota(jnp.int32, (PAIR, PAIR), 0)
    ci = lax.broadcasted_iota(jnp.int32, (PAIR, PAIR), 1)
    same_head = (ri // c) == (ci // c)
    strictly_lower = same_head & ((ri % c) > (ci % c))
    lower = same_head & ((ri % c) >= (ci % c))
    eye =(ri == ci).astype(F32)
    merge_masks = []
    s = 1
    while s < c:
        merge_masks.append(((ri // (2 * s)) == (ci // (2 * s)))
                           & ((ri // s) % 2 == 1) & ((ci // s) % 2 == 0))
        s *= 2

    def stack(t):
        return jnp.concatenate([jnp.where(even, t, 0.0), jnp.where(even, 0.0, t)], axis=0)

    def head_sum(t):
        s_even = jnp.sum(jnp.where(even, t, 0.0), axis=-1, keepdims=True)
        s_odd = jnp.sum(jnp.where(even, 0.0, t), axis=-1, keepdims=True)
        return jnp.where(even, s_even, s_odd)

    def bdot(a, b):
        return _dot(a.astype(BF16), b.astype(BF16))

    n_chunks = tw // c
    items = [(ic, pair) for ic in range(n_chunks) for pair in range(N_PAIRS)]

    def tile(ref_or_val, ic, pair):
        return ref_or_val[ic * c:(ic + 1) * c, pair * PAIR:(pair + 1) * PAIR]

    lw_all = lw_ref[...]
    bi = lax.broadcasted_iota(jnp.int32, (tw, tw), 0)
    bj = lax.broadcasted_iota(jnp.int32, (tw, tw), 1)
    cumsum_mat = (((bi // c) == (bj // c)) & (bi >= bj)).astype(BF16)
    cum_all = sum(_dot(cumsum_mat, piece) for piece in _split3(lw_all))

    prep = []
    for ic, pair in items:
        lw, cum = tile(lw_all, ic, pair), tile(cum_all, ic, pair)
        total = cum[c - 1:, :]
        kku = tile(kku_ref, ic, pair)
        kk = kku / jnp.maximum(jnp.sqrt(head_sum(kku * kku)), 1e-12)
        b_vec = kk * tile(a_ref, ic, pair)
        k_mod = tile(k_ref, ic, pair)
        r = tile(r_ref, ic, pair)
        v = tile(v_ref, ic, pair)
        w_inv = jnp.exp(-cum)
        w_rest = jnp.exp(total - cum)
        r_t = r * jnp.exp(cum)
        la = stack(-kk * jnp.exp(cum - lw))
        v_swapped = pltpu.roll(v, HEAD_DIM, 1)
        prep.append(dict(
            r=r, v=v, k_mod=k_mod, r_t=r_t, la=la, decay=jnp.exp(total),
            lhs=jnp.concatenate([la, stack(r_t)], axis=0).astype(BF16),
            rhs=jnp.concatenate([stack(b_vec * w_inv), stack(k_mod * w_inv)], axis=0).astype(BF16),
            lv_swapped=jnp.concatenate([jnp.where(even, 0.0, v_swapped),
                                        jnp.where(even, v_swapped, 0.0)], axis=0).astype(BF16),
            bk_rest=jnp.concatenate([b_vec * w_rest, k_mod * w_rest], axis=0).astype(BF16)))

    aas = [_dot_nt(it["lhs"], it["rhs"]) for it in prep]
    a_abs = [jnp.where(strictly_lower, aa[:PAIR, :PAIR], 0.0) for aa in aas]
    xs = [it["la"] + _dot(jnp.where(strictly_lower, aa[:PAIR, PAIR:], 0.0).astype(BF16), it["lv_swapped"])
          for it, aa in zip(prep, aas)]
    m_mix = [jnp.concatenate([jnp.where(lower, aa[PAIR:, :PAIR], 0.0),
                              jnp.where(lower, aa[PAIR:, PAIR:], 0.0)], axis=1).astype(BF16) for aa in aas]
    t_invs = [eye + jnp.where(merge_masks[0], a_ab, 0.0) for a_ab in a_abs]
    for mask in merge_masks[1:]:
        t_bf = [t.astype(BF16) for t in t_invs]
        inner = [_dot(jnp.where(mask, a_ab, 0.0).astype(BF16), t) for a_ab, t in zip(a_abs, t_bf)]
        t_invs = [t + _dot(tb, w.astype(BF16)) for t, tb, w in zip(t_invs, t_bf, inner)]
    xs = [bdot(t, x) for t, x in zip(t_invs, xs)]

    out_rows = []
    states = [st_ref[pair] for pair in range(N_PAIRS)]
    for ic in range(n_chunks):
        chunk = [(prep[ic * N_PAIRS + pair], xs[ic * N_PAIRS + pair], m_mix[ic * N_PAIRS + pair])
                 for pair in range(N_PAIRS)]
        p_mats = [jnp.where(even, x[:c], x[c:]) for _, x, _ in chunk]
        q_mats = [pltpu.roll(jnp.where(even, x[c:], x[:c]), HEAD_DIM, 1) for _, x, _ in chunk]
        urs = [_dot_nt(jnp.concatenate([p_mat, it["r_t"]], axis=0).astype(BF16), state.astype(BF16))
               for (it, _, _), p_mat, state in zip(chunk, p_mats, states)]
        us = [ur[:c] + q_mat for ur, q_mat in zip(urs, q_mats)]
        y_mixes = [_dot(mm, jnp.concatenate([stack(u), stack(it["v"])], axis=0).astype(BF16))
                   for (it, _, mm), u in zip(chunk, us)]
        updates = [_dot_tn(jnp.concatenate([u, it["v"]], axis=0).astype(BF16), it["bk_rest"])
                   for (it, _, _), u in zip(chunk, us)]
        states = [state * it["decay"] + jnp.where(same_head, update, 0.0)
                  for (it, _, _), state, update in zip(chunk, states, updates)]
        outs = []
        for pair, ((it, _, _), ur, y_mix) in enumerate(zip(chunk, urs, y_mixes)):
            cols = slice(pair * PAIR, (pair + 1) * PAIR)
            y = ur[c:] + y_mix[:c] + y_mix[c:]
            mean = head_sum(y) * (1.0 / HEAD_DIM)
            yc = y - mean
            var = head_sum(yc * yc) * (1.0 / HEAD_DIM)
            y_norm = yc * lax.rsqrt(var + GN_EPS) * lnw_ref[:, cols] + lnb_ref[:, cols]
            bonus = head_sum(it["r"] * it["k_mod"] * rk_ref[:, cols]) * it["v"]
            outs.append(((y_norm + bonus) * tile(g_ref, ic, pair)).astype(BF16))
        out_rows.append(jnp.concatenate(outs, axis=-1))
    for pair in range(N_PAIRS):
        st_ref[pair] = states[pair]
    o_ref[...] = jnp.concatenate(out_rows, axis=0)


def _wkv(r, lw, k_mod, v, kku, a, g, layer, seq_len, p):
    n_tok = r.shape[0]
    tw = TW_WKV
    tok = pl.BlockSpec((tw, D_GROUP), lambda i: (i, 0))
    vec = lambda arr: _layer_spec(arr, layer)
    return pl.pallas_call(
        functools.partial(_wkv_kernel, tw=tw, steps_per_seq=seq_len // tw),
        grid=(n_tok // tw,),
        in_specs=[tok] * 7 + [vec(p["r_k"]), vec(p["ln_w"]), vec(p["ln_b"])],
        out_specs=tok,
        out_shape=jax.ShapeDtypeStruct((n_tok, D_GROUP), BF16),
        scratch_shapes=[pltpu.VMEM((N_PAIRS, PAIR, PAIR), F32)],
        compiler_params=_params(1),
        name="wkv",
    )(r, lw, k_mod, v, kku, a, g, p["r_k"], p["ln_w"], p["ln_b"])


def _post_kernel(x_ref, attn_ref, yr_ref, wout_ref, gain_ref, wq_ref, kv_ref, wo_ref, o_ref):
    d = x_ref.shape[-1]
    dh = d // N_XATTN_HEADS
    mixed = jnp.concatenate([attn_ref[...], yr_ref[...]], axis=-1)
    x = x_ref[...] + _dot(mixed, wout_ref[...])
    h = _rms_norm(x, gain_ref[...]).astype(BF16)
    q = (_dot(h, wq_ref[...]) * dh ** -0.5).astype(BF16)
    heads = []
    for hd in range(N_XATTN_HEADS):
        s = _dot_nt(q[:, hd * dh:(hd + 1) * dh], kv_ref[:, hd * dh:(hd + 1) * dh])
        e = jnp.exp(s - jnp.max(s, axis=-1, keepdims=True))
        o = _dot(e.astype(BF16), kv_ref[:, d + hd * dh:d + (hd + 1) * dh])
        heads.append((o / jnp.sum(e, axis=-1, keepdims=True)).astype(BF16))
    o_ref[...] = x + _dot(jnp.concatenate(heads, axis=-1), wo_ref[...])


def _post(x2d, attn2d, yr, kv, layer, seq_len, p):
    n_tok, d = x2d.shape
    tm = TM_POST
    tiles_per_seq = seq_len // tm
    mem_len = kv.shape[2]
    tok = lambda width: pl.BlockSpec((tm, width), lambda i: (i, 0))
    return pl.pallas_call(
        _post_kernel,
        grid=(n_tok // tm,),
        in_specs=[tok(d), tok(D_GROUP), tok(D_GROUP), _layer_spec(p["w_out"], layer),
                  _layer_spec(p["norm_xattn"], layer), _layer_spec(p["wq"], layer),
                  pl.BlockSpec((None, None, mem_len, 2 * d), lambda i: (layer, i // tiles_per_seq, 0, 0)),
                  _layer_spec(p["wo"], layer)],
        out_specs=tok(d),
        out_shape=jax.ShapeDtypeStruct((n_tok, d), F32),
        compiler_params=_params(1),
        name="post",
    )(x2d, attn2d, yr, p["w_out"], p["norm_xattn"], p["wq"], kv, p["wo"])


def _ffn_kernel(*refs, final):
    x_ref, gain_ref, wg_ref, wu_ref, wd_ref = refs[:5]
    o_ref = refs[-1]
    x = x_ref[...]
    h = _rms_norm(x, gain_ref[...]).astype(BF16)
    d_ff = wg_ref.shape[-1]
    step = d_ff // FFN_CHUNKS
    y = x
    for ic in range(FFN_CHUNKS):
        cols = slice(ic * step, (ic + 1) * step)
        gate = _dot(h, wg_ref[:, cols])
        up = _dot(h, wu_ref[:, cols])
        act = (gate * jax.nn.sigmoid(gate) * up).astype(BF16)
        y = y + _dot(act, wd_ref[cols, :])
    if final:
        y = _rms_norm(y, refs[5][...])
    o_ref[...] = y


def _ffn(x2d, layer, p, final_gain):
    n_tok, d = x2d.shape
    tm = TM_FFN
    final = final_gain is not None
    tok = pl.BlockSpec((tm, d), lambda i: (i, 0))
    ins = [x2d, p["norm_ffn"], p["w_gate"], p["w_up"], p["w_down"]]
    in_specs = [tok] + [_layer_spec(a, layer) for a in ins[1:]]
    if final:
        ins.append(final_gain)
        in_specs.append(_const_spec(final_gain.shape, (0, 0)))
    return pl.pallas_call(
        functools.partial(_ffn_kernel, final=final),
        grid=(n_tok // tm,),
        in_specs=in_specs,
        out_specs=tok,
        out_shape=jax.ShapeDtypeStruct((n_tok, d), F32),
        compiler_params=_params(1),
        name="ffn",
    )(*ins)


def kernel(x, mem, mem_norm, norm_mix, w_in, attn_out_norm, rwkv_mu_rkv, rwkv_mu_lora, rwkv_mu_v, rwkv_w0, rwkv_w1, rwkv_w2, rwkv_a0, rwkv_a1, rwkv_a2, rwkv_v0, rwkv_v1, rwkv_v2, rwkv_g1, rwkv_g2, rwkv_k_k, rwkv_k_a, rwkv_r_k, rwkv_ln_w, rwkv_ln_b, w_out, norm_xattn, xattn_wq, xattn_wkv, xattn_wo, norm_ffn, ffn_w_gate, ffn_w_up, ffn_w_down, norm_final):
    bsz, seq, d = x.shape
    n_layers = w_in.shape[0]
    assert d == 2 * D_GROUP and seq % max(TM_MIX, TM_POST, TM_FFN, TW_WKV) == 0
    row = lambda t: t.reshape(t.shape[0], 1, -1)
    p = {
        "norm_mix": row(norm_mix), "w_in": w_in.astype(BF16),
        "mu_rkv": rwkv_mu_rkv, "mu_lora": rwkv_mu_lora, "mu_v": row(rwkv_mu_v),
        "w0": row(rwkv_w0), "w1": rwkv_w1.astype(BF16), "w2": rwkv_w2.astype(BF16),
        "a0": row(rwkv_a0), "a1": rwkv_a1.astype(BF16), "a2": rwkv_a2.astype(BF16),
        "v0": row(rwkv_v0), "v1": rwkv_v1.astype(BF16), "v2": rwkv_v2.astype(BF16),
        "g1": rwkv_g1.astype(BF16), "g2": rwkv_g2.astype(BF16),
        "k_k": row(rwkv_k_k), "k_a": row(rwkv_k_a), "r_k": row(rwkv_r_k),
        "ln_w": row(rwkv_ln_w), "ln_b": row(rwkv_ln_b),
        "w_out": w_out.astype(BF16), "norm_xattn": row(norm_xattn),
        "wq": xattn_wq.astype(BF16), "wo": xattn_wo.astype(BF16),
        "norm_ffn": row(norm_ffn), "w_gate": ffn_w_gate.astype(BF16),
        "w_up": ffn_w_up.astype(BF16), "w_down": ffn_w_down.astype(BF16),
    }
    attn_gain = row(attn_out_norm)
    mem_len = mem.shape[1]
    kv = _mem_kv(mem.reshape(bsz * mem_len, d), mem_norm.reshape(1, d), xattn_wkv.astype(BF16))
    kv = kv.reshape(n_layers, bsz, mem_len, 2 * d)

    x2d = x.reshape(bsz * seq, d)
    v_first = None
    for layer in range(n_layers):
        q, k, v, r, lw, k_mod, vv, kku, a, g = _mix_in(x2d, layer, seq, p, v_first)
        if layer == 0:
            v_first = vv
        attn = _dilated_attention(q, k, v, attn_gain, layer, seq)
        yr = _wkv(r, lw, k_mod, vv, kku, a, g, layer, seq, p)
        x2d = _post(x2d, attn, yr, kv, layer, seq, p)
        x2d = _ffn(x2d, layer, p, norm_final.reshape(1, d) if layer == n_layers - 1 else None)
    return x2d.reshape(bsz, seq, d)
```

```python
import functools

import jax
import jax.numpy as jnp
from jax import lax
from jax.experimental import pallas as pl
from jax.experimental.pallas import tpu as pltpu

F32 = jnp.float32
BF16 = jnp.bfloat16

HEAD_DIM = 64
PAIR = 2 * HEAD_DIM
D_GROUP = 512
N_PAIRS = D_GROUP // PAIR
DILATED_PATTERNS = ((128, 1), (512, 4), (2048, 16))
ATTN_BLOCK = 128
ATTN_SPAN = ATTN_BLOCK * max(d for _, d in DILATED_PATTERNS)
ATTN_BLOCKS_PER_STEP = 2
ATTN_NORM_ROWS = 256
N_XATTN_HEADS = 4
NORM_EPS = 1e-6
GN_EPS = 64e-5
MASK_VALUE = -1e30
WKV_CHUNK = 64

V7X_VMEM_LIMIT_BYTES = 56 * 1024 * 1024
TM_MIX = 512
TM_POST = 512
TM_FFN = 512
TW_WKV = 256
FFN_CHUNKS = 2


def _dot(a, b):
    return lax.dot_general(a, b, (((1,), (0,)), ((), ())), preferred_element_type=F32)


def _dot_nt(a, b):
    return lax.dot_general(a, b, (((1,), (1,)), ((), ())), preferred_element_type=F32)


def _dot_tn(a, b):
    return lax.dot_general(a, b, (((0,), (0,)), ((), ())), preferred_element_type=F32)


def _rms_norm(x, gain):
    return x * lax.rsqrt(jnp.mean(x * x, axis=-1, keepdims=True) + NORM_EPS) * gain


def _softplus(z):
    return jnp.maximum(z, 0.0) + jnp.log1p(jnp.exp(-jnp.abs(z)))


def _params(n_grid_dims):
    return pltpu.CompilerParams(dimension_semantics=("arbitrary",) * n_grid_dims,
                                vmem_limit_bytes=V7X_VMEM_LIMIT_BYTES)


def _const_spec(shape, index):
    return pl.BlockSpec(shape, lambda *_: index, pipeline_mode=pl.Buffered(1))


def _layer_spec(arr, layer):
    return _const_spec((None,) + arr.shape[1:], (layer,) + (0,) * (arr.ndim - 1))


def _mem_kv_kernel(mem_ref, gain_ref, wkv_ref, o_ref):
    mem_n = _rms_norm(mem_ref[...], gain_ref[...]).astype(BF16)
    o_ref[...] = _dot(mem_n, wkv_ref[...]).astype(BF16)


def _mem_kv(mem2d, gain, wkv):
    n_layers, d, d2 = wkv.shape
    rows = mem2d.shape[0]
    return pl.pallas_call(
        _mem_kv_kernel,
        grid=(n_layers,),
        in_specs=[pl.BlockSpec((rows, d), lambda l: (0, 0)),
                  pl.BlockSpec((1, d), lambda l: (0, 0)),
                  pl.BlockSpec((None, d, d2), lambda l: (l, 0, 0))],
        out_specs=pl.BlockSpec((None, rows, d2), lambda l: (l, 0, 0)),
        out_shape=jax.ShapeDtypeStruct((n_layers, rows, d2), BF16),
        compiler_params=_params(1),
        name="mem_kv",
    )(mem2d, gain, wkv)


def _mix_in_kernel(*refs, has_vres, tm, tiles_per_seq):
    (x_ref, gain_ref, w_in_ref, mu_rkv_ref, mu_lora_ref, w0_ref, w1_ref, w2_ref, a0_ref, a1_ref,
     a2_ref, g1_ref, g2_ref, kk_ref, ka_ref) = refs[:15]
    rest = refs[15:]
    if has_vres:
        mu_v_ref, v0_ref, v1_ref, v2_ref, vfirst_ref = rest[:5]
        rest = rest[5:]
    (q_out, k_out, v_out, r_out, lw_out, km_out, vv_out, kku_out, a_out, g_out,
     carry_h, carry_p) = rest

    @pl.when(pl.program_id(0) % tiles_per_seq == 0)
    def _start_of_sequence():
        carry_h[...] = jnp.zeros_like(carry_h)
        carry_p[...] = jnp.zeros_like(carry_p)

    h = _rms_norm(x_ref[...], gain_ref[...])
    proj = _dot(h.astype(BF16), w_in_ref[...])
    q_out[...] = (proj[:, 0:D_GROUP] * HEAD_DIM ** -0.5).astype(BF16)
    k_out[...] = proj[:, D_GROUP:2 * D_GROUP].astype(BF16)
    v_out[...] = proj[:, 2 * D_GROUP:3 * D_GROUP].astype(BF16)
    rkv = proj[:, 3 * D_GROUP:]

    row = lax.broadcasted_iota(jnp.int32, (tm, 1), 0)

    def minus_previous_token(t, carry_ref):
        prev = jnp.where(row == 0, carry_ref[...], pltpu.roll(t, 1, 0))
        carry_ref[...] = t[tm - 1:, :]
        return prev - t

    dh = minus_previous_token(h, carry_h)
    drkv = minus_previous_token(rkv, carry_p)

    mu_rkv = mu_rkv_ref[...]
    r = rkv[:, 0:D_GROUP] + drkv[:, 0:D_GROUP] * mu_rkv[0:1]
    k = rkv[:, D_GROUP:2 * D_GROUP] + drkv[:, D_GROUP:2 * D_GROUP] * mu_rkv[1:2]
    v = rkv[:, 2 * D_GROUP:] + drkv[:, 2 * D_GROUP:] * mu_rkv[2:3]

    mu_lora = mu_lora_ref[...]

    def lora(mu_row, w_first_ref, act, w_second_ref):
        hidden = _dot((h + dh * mu_row).astype(BF16), w_first_ref[...])
        return _dot(act(hidden).astype(BF16), w_second_ref[...])

    w_log = -_softplus(-(w0_ref[...] + lora(mu_lora[0:1], w1_ref, jnp.tanh, w2_ref))) - 0.5
    lw_out[...] = -jnp.exp(w_log)
    a = jax.nn.sigmoid(a0_ref[...] + lora(mu_lora[1:2], a1_ref, lambda t: t, a2_ref))
    g_out[...] = lora(mu_lora[2:3], g1_ref, jax.nn.sigmoid, g2_ref)
    if has_vres:
        gate = jax.nn.sigmoid(v0_ref[...] + lora(mu_v_ref[...], v1_ref, lambda t: t, v2_ref))
        v = v + (vfirst_ref[...] - v) * gate

    r_out[...] = r
    vv_out[...] = v
    a_out[...] = a
    kku_out[...] = k * kk_ref[...]
    km_out[...] = k * (1.0 + (a - 1.0) * ka_ref[...])


def _mix_in(x2d, layer, seq_len, p, v_first):
    n_tok, d = x2d.shape
    tm = TM_MIX
    has_vres = layer > 0
    tok = lambda width: pl.BlockSpec((tm, width), lambda i: (i, 0))
    ins = [x2d, p["norm_mix"], p["w_in"], p["mu_rkv"], p["mu_lora"], p["w0"], p["w1"], p["w2"],
           p["a0"], p["a1"], p["a2"], p["g1"], p["g2"], p["k_k"], p["k_a"]]
    in_specs = [tok(d)] + [_layer_spec(a, layer) for a in ins[1:]]
    if has_vres:
        extra = [p["mu_v"], p["v0"], p["v1"], p["v2"]]
        ins += extra + [v_first]
        in_specs += [_layer_spec(a, layer - 1) for a in extra] + [tok(D_GROUP)]
    out_shape = ([jax.ShapeDtypeStruct((n_tok, D_GROUP), BF16)] * 3
                 + [jax.ShapeDtypeStruct((n_tok, D_GROUP), F32)] * 7)
    return pl.pallas_call(
        functools.partial(_mix_in_kernel, has_vres=has_vres, tm=tm, tiles_per_seq=seq_len // tm),
        grid=(n_tok // tm,),
        in_specs=in_specs,
        out_specs=[tok(D_GROUP)] * 10,
        out_shape=out_shape,
        scratch_shapes=[pltpu.VMEM((1, d), F32), pltpu.VMEM((1, 3 * D_GROUP), F32)],
        compiler_params=_params(1),
        name="mix_in",
    )(*ins)


def _attn_kernel(q_ref, k_ref, v_ref, gain_ref, o_ref, q_st, kv_st, acc_ref, stat_ref, *,
                 spans_per_seq, patterns):
    blk, span = ATTN_BLOCK, ATTN_SPAN
    j = pl.program_id(0) % spans_per_seq
    cur_base = (j % 2) * span
    prev_base = span - cur_base

    @pl.when(j == 0)
    def _no_previous_span():
        kv_st[:, pl.ds(prev_base, span), :] = jnp.zeros((2 * N_PAIRS, span, PAIR), F32)

    for pair in range(N_PAIRS):
        cols = slice(pair * PAIR, (pair + 1) * PAIR)
        q_st[pair] = q_ref[:, cols].astype(F32)
        kv_st[pair, pl.ds(cur_base, span), :] = k_ref[:, cols].astype(F32)
        kv_st[N_PAIRS + pair, pl.ds(cur_base, span), :] = v_ref[:, cols].astype(F32)

    qi = lax.broadcasted_iota(jnp.int32, (2 * blk, 2 * blk), 0) % blk
    ki = lax.broadcasted_iota(jnp.int32, (2 * blk, 2 * blk), 1)
    dist = qi + blk - ki
    even = lax.broadcasted_iota(jnp.int32, (1, PAIR), 1) < HEAD_DIM
    ones = jnp.ones((2 * blk, PAIR), BF16)
    pairs = range(N_PAIRS)

    def run_pass(n_back, dil, first, last):
        blocks_per_sub = span // (blk * dil)
        shift = blocks_per_sub.bit_length() - 1
        stride = None if dil == 1 else dil

        def rows(start):
            return pl.ds(start, blk, stride=stride) if stride else pl.ds(pl.multiple_of(start, blk), blk)

        def by_head(stacked):
            return jnp.where(even, stacked[:blk], stacked[blk:])

        def blocks_step(step, carry):
            starts, valids, items = [], [], []
            for ib in range(ATTN_BLOCKS_PER_STEP):
                t = step * ATTN_BLOCKS_PER_STEP + ib
                r = lax.shift_right_logical(t, shift)
                n = t & (blocks_per_sub - 1)
                start = n * (blk * dil) + r
                cur = cur_base + start
                prev = jnp.where(n > 0, cur - blk * dil, prev_base + span - blk * dil + r)
                first_key = jnp.where((n > 0) | (j > 0), 0, blk)
                starts.append(start)
                valids.append((dist >= 0) & (dist <= n_back) & (ki >= first_key))
                items += [(ib, p, cur, prev) for p in pairs]

            if not first:
                m_prev = [stat_ref[p, rows(starts[ib]), :] for ib, p, _, _ in items]
                l_prev = [stat_ref[N_PAIRS + p, rows(starts[ib]), :] for ib, p, _, _ in items]
                acc_prev = [acc_ref[p, rows(starts[ib]), :] for ib, p, _, _ in items]
            zero = jnp.zeros((blk, PAIR), BF16)
            work = [dict() for _ in items]

            def scores(i):
                ib, p, cur, prev = items[i]
                q = q_st[p, rows(starts[ib]), :].astype(BF16)
                qq = jnp.concatenate([jnp.where(even, q, zero), jnp.where(even, zero, q)], axis=0)
                key = jnp.concatenate([kv_st[p, rows(prev), :].astype(BF16),
                                       kv_st[p, rows(cur), :].astype(BF16)], axis=0)
                work[i]["s"] = jnp.where(valids[ib], _dot_nt(qq, key), MASK_VALUE)

            def weights(i):
                s = work[i].pop("s")
                m_new = jnp.broadcast_to(jnp.max(jnp.maximum(s[:, :blk], s[:, blk:]), axis=-1, keepdims=True),
                                         (2 * blk, PAIR))
                if not first:
                    swapped = pltpu.roll(m_prev[i], HEAD_DIM, 1)
                    m_old = jnp.concatenate([jnp.where(even, m_prev[i], swapped),
                                             jnp.where(even, swapped, m_prev[i])], axis=0)
                    m_new = jnp.maximum(m_new, m_old)
                    work[i]["alpha"] = by_head(jnp.exp(m_old - m_new))
                work[i]["m"] = by_head(m_new)
                work[i]["p"] = jnp.exp(s - jnp.concatenate([m_new, m_new], axis=1)).astype(BF16)

            def accumulate(i):
                ib, p, cur, prev = items[i]
                val = jnp.concatenate([kv_st[N_PAIRS + p, rows(prev), :].astype(BF16),
                                       kv_st[N_PAIRS + p, rows(cur), :].astype(BF16)], axis=0)
                pv = _dot(work[i].pop("p"), jnp.concatenate([val, ones], axis=1))
                num, l_new = by_head(pv[:, :PAIR]), by_head(pv[:, PAIR:])
                if not first:
                    alpha = work[i].pop("alpha")
                    l_new = alpha * l_prev[i] + l_new
                    num = alpha * acc_prev[i] + num
                if last:
                    acc_ref[p, rows(starts[ib]), :] = num / l_new
                else:
                    acc_ref[p, rows(starts[ib]), :] = num
                    stat_ref[p, rows(starts[ib]), :] = work[i].pop("m")
                    stat_ref[N_PAIRS + p, rows(starts[ib]), :] = l_new

            for stage in (scores, weights, accumulate):
                for i in range(len(items)):
                    stage(i)
            return carry

        lax.fori_loop(0, span // (blk * ATTN_BLOCKS_PER_STEP), blocks_step, 0)

    for idx, (n_back, dil) in enumerate(patterns):
        run_pass(n_back, dil, first=idx == 0, last=idx == len(patterns) - 1)

    def normalise(ic, carry):
        rows = pl.ds(pl.multiple_of(ic * ATTN_NORM_ROWS, ATTN_NORM_ROWS), ATTN_NORM_ROWS)
        out = jnp.concatenate([acc_ref[p, rows, :] for p in pairs], axis=-1)
        o_ref[rows, :] = _rms_norm(out, gain_ref[...]).astype(BF16)
        return carry

    lax.fori_loop(0, span // ATTN_NORM_ROWS, normalise, 0)


def _dilated_attention(q, k, v, gain, layer, seq_len):
    n_tok, width = q.shape
    assert seq_len % ATTN_SPAN == 0, "sequence length must be a multiple of the largest dilated span"
    patterns = tuple(sorted(((w // d, d) for w, d in DILATED_PATTERNS), key=lambda nd: -nd[1]))
    assert patterns[-1][1] == 1 and all(ATTN_SPAN % (ATTN_BLOCK * d) == 0 for _, d in patterns)
    assert all(n_back <= ATTN_BLOCK for n_back, _ in patterns), \
        "a query block may only reach into the previous key block"
    tok = pl.BlockSpec((ATTN_SPAN, width), lambda i: (i, 0))
    tok_in = pl.BlockSpec((ATTN_SPAN, width), lambda i: (i, 0), pipeline_mode=pl.Buffered(1))
    return pl.pallas_call(
        functools.partial(_attn_kernel, spans_per_seq=seq_len // ATTN_SPAN, patterns=patterns),
        grid=(n_tok // ATTN_SPAN,),
        in_specs=[tok_in, tok_in, tok_in, _layer_spec(gain, layer)],
        out_specs=tok,
        out_shape=jax.ShapeDtypeStruct((n_tok, width), BF16),
        scratch_shapes=[pltpu.VMEM((N_PAIRS, ATTN_SPAN, PAIR), F32),
                        pltpu.VMEM((2 * N_PAIRS, 2 * ATTN_SPAN, PAIR), F32),
                        pltpu.VMEM((N_PAIRS, ATTN_SPAN, PAIR), F32),
                        pltpu.VMEM((2 * N_PAIRS, ATTN_SPAN, PAIR), F32)],
        compiler_params=_params(1),
        name="attn",
    )(q, k, v, gain)


def _split3(x):
    hi = x.astype(BF16)
    r1 = x - hi.astype(F32)
    mid = r1.astype(BF16)
    lo = (r1 - mid.astype(F32)).astype(BF16)
    return hi, mid, lo


def _wkv_kernel(r_ref, lw_ref, k_ref, v_ref, kku_ref, a_ref, g_ref, rk_ref, lnw_ref, lnb_ref,
                o_ref, st_ref, *, tw, steps_per_seq):
    @pl.when(pl.program_id(0) % steps_per_seq == 0)
    def _start_of_sequence():
        st_ref[...] = jnp.zeros_like(st_ref)

    c = WKV_CHUNK
    even = lax.broadcasted_iota(jnp.int32, (1, PAIR), 1) < HEAD_DIM
    ri = lax.broadcasted_iota(jnp.int32, (PAIR, PAIR), 0)
    ci = lax.broadcasted_iota(jnp.int32, (PAIR, PAIR), 1)
    same_head = (ri // c) == (ci // c)
    strictly_lower = same_head & ((ri % c) > (ci % c))
    lower = same_head & ((ri % c) >= (ci % c))
    eye =(ri == ci).astype(F32)
    merge_masks = []
    s = 1
    while s < c:
        merge_masks.append(((ri // (2 * s)) == (ci // (2 * s)))
                           & ((ri // s) % 2 == 1) & ((ci // s) % 2 == 0))
        s *= 2

    def stack(t):
        return jnp.concatenate([jnp.where(even, t, 0.0), jnp.where(even, 0.0, t)], axis=0)

    def head_sum(t):
        s_even = jnp.sum(jnp.where(even, t, 0.0), axis=-1, keepdims=True)
        s_odd = jnp.sum(jnp.where(even, 0.0, t), axis=-1, keepdims=True)
        return jnp.where(even, s_even, s_odd)

    def bdot(a, b):
        return _dot(a.astype(BF16), b.astype(BF16))

    n_chunks = tw // c
    items = [(ic, pair) for ic in range(n_chunks) for pair in range(N_PAIRS)]

    def tile(ref_or_val, ic, pair):
        return ref_or_val[ic * c:(ic + 1) * c, pair * PAIR:(pair + 1) * PAIR]

    lw_all = lw_ref[...]
    bi = lax.broadcasted_iota(jnp.int32, (tw, tw), 0)
    bj = lax.broadcasted_iota(jnp.int32, (tw, tw), 1)
    cumsum_mat = (((bi // c) == (bj // c)) & (bi >= bj)).astype(BF16)
    cum_all = sum(_dot(cumsum_mat, piece) for piece in _split3(lw_all))

    prep = []
    for ic, pair in items:
        lw, cum = tile(lw_all, ic, pair), tile(cum_all, ic, pair)
        total = cum[c - 1:, :]
        kku = tile(kku_ref, ic, pair)
        kk = kku / jnp.maximum(jnp.sqrt(head_sum(kku * kku)), 1e-12)
        b_vec = kk * tile(a_ref, ic, pair)
        k_mod = tile(k_ref, ic, pair)
        r = tile(r_ref, ic, pair)
        v = tile(v_ref, ic, pair)
        w_inv = jnp.exp(-cum)
        w_rest = jnp.exp(total - cum)
        r_t = r * jnp.exp(cum)
        la = stack(-kk * jnp.exp(cum - lw))
        v_swapped = pltpu.roll(v, HEAD_DIM, 1)
        prep.append(dict(
            r=r, v=v, k_mod=k_mod, r_t=r_t, la=la, decay=jnp.exp(total),
            lhs=jnp.concatenate([la, stack(r_t)], axis=0).astype(BF16),
            rhs=jnp.concatenate([stack(b_vec * w_inv), stack(k_mod * w_inv)], axis=0).astype(BF16),
            lv_swapped=jnp.concatenate([jnp.where(even, 0.0, v_swapped),
                                        jnp.where(even, v_swapped, 0.0)], axis=0).astype(BF16),
            bk_rest=jnp.concatenate([b_vec * w_rest, k_mod * w_rest], axis=0).astype(BF16)))

    aas = [_dot_nt(it["lhs"], it["rhs"]) for it in prep]
    a_abs = [jnp.where(strictly_lower, aa[:PAIR, :PAIR], 0.0) for aa in aas]
    xs = [it["la"] + _dot(jnp.where(strictly_lower, aa[:PAIR, PAIR:], 0.0).astype(BF16), it["lv_swapped"])
          for it, aa in zip(prep, aas)]
    m_mix = [jnp.concatenate([jnp.where(lower, aa[PAIR:, :PAIR], 0.0),
                              jnp.where(lower, aa[PAIR:, PAIR:], 0.0)], axis=1).astype(BF16) for aa in aas]
    t_invs = [eye + jnp.where(merge_masks[0], a_ab, 0.0) for a_ab in a_abs]
    for mask in merge_masks[1:]:
        t_bf = [t.astype(BF16) for t in t_invs]
        inner = [_dot(jnp.where(mask, a_ab, 0.0).astype(BF16), t) for a_ab, t in zip(a_abs, t_bf)]
        t_invs = [t + _dot(tb, w.astype(BF16)) for t, tb, w in zip(t_invs, t_bf, inner)]
    xs = [bdot(t, x) for t, x in zip(t_invs, xs)]

    out_rows = []
    states = [st_ref[pair] for pair in range(N_PAIRS)]
    for ic in range(n_chunks):
        chunk = [(prep[ic * N_PAIRS + pair], xs[ic * N_PAIRS + pair], m_mix[ic * N_PAIRS + pair])
                 for pair in range(N_PAIRS)]
        p_mats = [jnp.where(even, x[:c], x[c:]) for _, x, _ in chunk]
        q_mats = [pltpu.roll(jnp.where(even, x[c:], x[:c]), HEAD_DIM, 1) for _, x, _ in chunk]
        urs = [_dot_nt(jnp.concatenate([p_mat, it["r_t"]], axis=0).astype(BF16), state.astype(BF16))
               for (it, _, _), p_mat, state in zip(chunk, p_mats, states)]
        us = [ur[:c] + q_mat for ur, q_mat in zip(urs, q_mats)]
        y_mixes = [_dot(mm, jnp.concatenate([stack(u), stack(it["v"])], axis=0).astype(BF16))
                   for (it, _, mm), u in zip(chunk, us)]
        updates = [_dot_tn(jnp.concatenate([u, it["v"]], axis=0).astype(BF16), it["bk_rest"])
                   for (it, _, _), u in zip(chunk, us)]
        states = [state * it["decay"] + jnp.where(same_head, update, 0.0)
                  for (it, _, _), state, update in zip(chunk, states, updates)]
        outs = []
        for pair, ((it, _, _), ur, y_mix) in enumerate(zip(chunk, urs, y_mixes)):
            cols = slice(pair * PAIR, (pair + 1) * PAIR)
            y = ur[c:] + y_mix[:c] + y_mix[c:]
            mean = head_sum(y) * (1.0 / HEAD_DIM)
            yc = y - mean
            var = head_sum(yc * yc) * (1.0 / HEAD_DIM)
            y_norm = yc * lax.rsqrt(var + GN_EPS) * lnw_ref[:, cols] + lnb_ref[:, cols]
            bonus = head_sum(it["r"] * it["k_mod"] * rk_ref[:, cols]) * it["v"]
            outs.append(((y_norm + bonus) * tile(g_ref, ic, pair)).astype(BF16))
        out_rows.append(jnp.concatenate(outs, axis=-1))
    for pair in range(N_PAIRS):
        st_ref[pair] = states[pair]
    o_ref[...] = jnp.concatenate(out_rows, axis=0)


def _wkv(r, lw, k_mod, v, kku, a, g, layer, seq_len, p):
    n_tok = r.shape[0]
    tw = TW_WKV
    tok = pl.BlockSpec((tw, D_GROUP), lambda i: (i, 0))
    vec = lambda arr: _layer_spec(arr, layer)
    return pl.pallas_call(
        functools.partial(_wkv_kernel, tw=tw, steps_per_seq=seq_len // tw),
        grid=(n_tok // tw,),
        in_specs=[tok] * 7 + [vec(p["r_k"]), vec(p["ln_w"]), vec(p["ln_b"])],
        out_specs=tok,
        out_shape=jax.ShapeDtypeStruct((n_tok, D_GROUP), BF16),
        scratch_shapes=[pltpu.VMEM((N_PAIRS, PAIR, PAIR), F32)],
        compiler_params=_params(1),
        name="wkv",
    )(r, lw, k_mod, v, kku, a, g, p["r_k"], p["ln_w"], p["ln_b"])


def _post_kernel(x_ref, attn_ref, yr_ref, wout_ref, gain_ref, wq_ref, kv_ref, wo_ref, o_ref):
    d = x_ref.shape[-1]
    dh = d // N_XATTN_HEADS
    mixed = jnp.concatenate([attn_ref[...], yr_ref[...]], axis=-1)
    x = x_ref[...] + _dot(mixed, wout_ref[...])
    h = _rms_norm(x, gain_ref[...]).astype(BF16)
    q = (_dot(h, wq_ref[...]) * dh ** -0.5).astype(BF16)
    heads = []
    for hd in range(N_XATTN_HEADS):
        s = _dot_nt(q[:, hd * dh:(hd + 1) * dh], kv_ref[:, hd * dh:(hd + 1) * dh])
        e = jnp.exp(s - jnp.max(s, axis=-1, keepdims=True))
        o = _dot(e.astype(BF16), kv_ref[:, d + hd * dh:d + (hd + 1) * dh])
        heads.append((o / jnp.sum(e, axis=-1, keepdims=True)).astype(BF16))
    o_ref[...] = x + _dot(jnp.concatenate(heads, axis=-1), wo_ref[...])


def _post(x2d, attn2d, yr, kv, layer, seq_len, p):
    n_tok, d = x2d.shape
    tm = TM_POST
    tiles_per_seq = seq_len // tm
    mem_len = kv.shape[2]
    tok = lambda width: pl.BlockSpec((tm, width), lambda i: (i, 0))
    return pl.pallas_call(
        _post_kernel,
        grid=(n_tok // tm,),
        in_specs=[tok(d), tok(D_GROUP), tok(D_GROUP), _layer_spec(p["w_out"], layer),
                  _layer_spec(p["norm_xattn"], layer), _layer_spec(p["wq"], layer),
                  pl.BlockSpec((None, None, mem_len, 2 * d), lambda i: (layer, i // tiles_per_seq, 0, 0)),
                  _layer_spec(p["wo"], layer)],
        out_specs=tok(d),
        out_shape=jax.ShapeDtypeStruct((n_tok, d), F32),
        compiler_params=_params(1),
        name="post",
    )(x2d, attn2d, yr, p["w_out"], p["norm_xattn"], p["wq"], kv, p["wo"])


def _ffn_kernel(*refs, final):
    x_ref, gain_ref, wg_ref, wu_ref, wd_ref = refs[:5]
    o_ref = refs[-1]
    x = x_ref[...]
    h = _rms_norm(x, gain_ref[...]).astype(BF16)
    d_ff = wg_ref.shape[-1]
    step = d_ff // FFN_CHUNKS
    y = x
    for ic in range(FFN_CHUNKS):
        cols = slice(ic * step, (ic + 1) * step)
        gate = _dot(h, wg_ref[:, cols])
        up = _dot(h, wu_ref[:, cols])
        act = (gate * jax.nn.sigmoid(gate) * up).astype(BF16)
        y = y + _dot(act, wd_ref[cols, :])
    if final:
        y = _rms_norm(y, refs[5][...])
    o_ref[...] = y


def _ffn(x2d, layer, p, final_gain):
    n_tok, d = x2d.shape
    tm = TM_FFN
    final = final_gain is not None
    tok = pl.BlockSpec((tm, d), lambda i: (i, 0))
    ins = [x2d, p["norm_ffn"], p["w_gate"], p["w_up"], p["w_down"]]
    in_specs = [tok] + [_layer_spec(a, layer) for a in ins[1:]]
    if final:
        ins.append(final_gain)
        in_specs.append(_const_spec(final_gain.shape, (0, 0)))
    return pl.pallas_call(
        functools.partial(_ffn_kernel, final=final),
        grid=(n_tok // tm,),
        in_specs=in_specs,
        out_specs=tok,
        out_shape=jax.ShapeDtypeStruct((n_tok, d), F32),
        compiler_params=_params(1),
        name="ffn",
    )(*ins)


def kernel(x, mem, mem_norm, norm_mix, w_in, attn_out_norm, rwkv_mu_rkv, rwkv_mu_lora, rwkv_mu_v, rwkv_w0, rwkv_w1, rwkv_w2, rwkv_a0, rwkv_a1, rwkv_a2, rwkv_v0, rwkv_v1, rwkv_v2, rwkv_g1, rwkv_g2, rwkv_k_k, rwkv_k_a, rwkv_r_k, rwkv_ln_w, rwkv_ln_b, w_out, norm_xattn, xattn_wq, xattn_wkv, xattn_wo, norm_ffn, ffn_w_gate, ffn_w_up, ffn_w_down, norm_final):
    bsz, seq, d = x.shape
    n_layers = w_in.shape[0]
    assert d == 2 * D_GROUP and seq % max(TM_MIX, TM_POST, TM_FFN, TW_WKV) == 0
    row = lambda t: t.reshape(t.shape[0], 1, -1)
    p = {
        "norm_mix": row(norm_mix), "w_in": w_in.astype(BF16),
        "mu_rkv": rwkv_mu_rkv, "mu_lora": rwkv_mu_lora, "mu_v": row(rwkv_mu_v),
        "w0": row(rwkv_w0), "w1": rwkv_w1.astype(BF16), "w2": rwkv_w2.astype(BF16),
        "a0": row(rwkv_a0), "a1": rwkv_a1.astype(BF16), "a2": rwkv_a2.astype(BF16),
        "v0": row(rwkv_v0), "v1": rwkv_v1.astype(BF16), "v2": rwkv_v2.astype(BF16),
        "g1": rwkv_g1.astype(BF16), "g2": rwkv_g2.astype(BF16),
        "k_k": row(rwkv_k_k), "k_a": row(rwkv_k_a), "r_k": row(rwkv_r_k),
        "ln_w": row(rwkv_ln_w), "ln_b": row(rwkv_ln_b),
        "w_out": w_out.astype(BF16), "norm_xattn": row(norm_xattn),
        "wq": xattn_wq.astype(BF16), "wo": xattn_wo.astype(BF16),
        "norm_ffn": row(norm_ffn), "w_gate": ffn_w_gate.astype(BF16),
        "w_up": ffn_w_up.astype(BF16), "w_down": ffn_w_down.astype(BF16),
    }
    attn_gain = row(attn_out_norm)
    mem_len = mem.shape[1]
    kv = _mem_kv(mem.reshape(bsz * mem_len, d), mem_norm.reshape(1, d), xattn_wkv.astype(BF16))
    kv = kv.reshape(n_layers, bsz, mem_len, 2 * d)

    x2d = x.reshape(bsz * seq, d)
    v_first = None
    for layer in range(n_layers):
        q, k, v, r, lw, k_mod, vv, kku, a, g = _mix_in(x2d, layer, seq, p, v_first)
        if layer == 0:
            v_first = vv
        attn = _dilated_attention(q, k, v, attn_gain, layer, seq)
        yr = _wkv(r, lw, k_mod, vv, kku, a, g, layer, seq, p)
        x2d = _post(x2d, attn, yr, kv, layer, seq, p)
        x2d = _ffn(x2d, layer, p, norm_final.reshape(1, d) if layer == n_layers - 1 else None)
    return x2d.reshape(bsz, seq, d)
```

```python
import functools

import jax
import jax.numpy as jnp
from jax import lax
from jax.experimental import pallas as pl
from jax.experimental.pallas import tpu as pltpu

F32 = jnp.float32
BF16 = jnp.bfloat16

HEAD_DIM = 64
PAIR = 2 * HEAD_DIM
D_GROUP = 512
N_PAIRS = D_GROUP // PAIR
DILATED_PATTERNS = ((128, 1), (512, 4), (2048, 16))
ATTN_BLOCK = 128
ATTN_SPAN = ATTN_BLOCK * max(d for _, d in DILATED_PATTERNS)
ATTN_BLOCKS_PER_STEP = 2
ATTN_NORM_ROWS = 256
N_XATTN_HEADS = 4
NORM_EPS = 1e-6
GN_EPS = 64e-5
MASK_VALUE = -1e30
WKV_CHUNK = 64

V7X_VMEM_LIMIT_BYTES = 56 * 1024 * 1024
TM_MIX = 512
TM_POST = 512
TM_FFN = 512
TW_WKV = 256
FFN_CHUNKS = 2


def _dot(a, b):
    return lax.dot_general(a, b, (((1,), (0,)), ((), ())), preferred_element_type=F32)


def _dot_nt(a, b):
    return lax.dot_general(a, b, (((1,), (1,)), ((), ())), preferred_element_type=F32)


def _dot_tn(a, b):
    return lax.dot_general(a, b, (((0,), (0,)), ((), ())), preferred_element_type=F32)


def _rms_norm(x, gain):
    return x * lax.rsqrt(jnp.mean(x * x, axis=-1, keepdims=True) + NORM_EPS) * gain


def _softplus(z):
    return jnp.maximum(z, 0.0) + jnp.log1p(jnp.exp(-jnp.abs(z)))


def _params(n_grid_dims):
    return pltpu.CompilerParams(dimension_semantics=("arbitrary",) * n_grid_dims,
                                vmem_limit_bytes=V7X_VMEM_LIMIT_BYTES)


def _const_spec(shape, index):
    return pl.BlockSpec(shape, lambda *_: index, pipeline_mode=pl.Buffered(1))


def _layer_spec(arr, layer):
    return _const_spec((None,) + arr.shape[1:], (layer,) + (0,) * (arr.ndim - 1))


def _mem_kv_kernel(mem_ref, gain_ref, wkv_ref, o_ref):
    mem_n = _rms_norm(mem_ref[...], gain_ref[...]).astype(BF16)
    o_ref[...] = _dot(mem_n, wkv_ref[...]).astype(BF16)


def _mem_kv(mem2d, gain, wkv):
    n_layers, d, d2 = wkv.shape
    rows = mem2d.shape[0]
    return pl.pallas_call(
        _mem_kv_kernel,
        grid=(n_layers,),
        in_specs=[pl.BlockSpec((rows, d), lambda l: (0, 0)),
                  pl.BlockSpec((1, d), lambda l: (0, 0)),
                  pl.BlockSpec((None, d, d2), lambda l: (l, 0, 0))],
        out_specs=pl.BlockSpec((None, rows, d2), lambda l: (l, 0, 0)),
        out_shape=jax.ShapeDtypeStruct((n_layers, rows, d2), BF16),
        compiler_params=_params(1),
        name="mem_kv",
    )(mem2d, gain, wkv)


def _mix_in_kernel(*refs, has_vres, tm, tiles_per_seq):
    (x_ref, gain_ref, w_in_ref, mu_rkv_ref, mu_lora_ref, w0_ref, w1_ref, w2_ref, a0_ref, a1_ref,
     a2_ref, g1_ref, g2_ref, kk_ref, ka_ref) = refs[:15]
    rest = refs[15:]
    if has_vres:
        mu_v_ref, v0_ref, v1_ref, v2_ref, vfirst_ref = rest[:5]
        rest = rest[5:]
    (q_out, k_out, v_out, r_out, lw_out, km_out, vv_out, kku_out, a_out, g_out,
     carry_h, carry_p) = rest

    @pl.when(pl.program_id(0) % tiles_per_seq == 0)
    def _start_of_sequence():
        carry_h[...] = jnp.zeros_like(carry_h)
        carry_p[...] = jnp.zeros_like(carry_p)

    row = lax.broadcasted_iota(jnp.int32, (tm, 1), 0)

    def minus_previous_token(t, carry_ref):
        prev = jnp.where(row == 0, carry_ref[...], pltpu.roll(t, 1, 0))
        carry_ref[...] = t[tm - 1:, :]
        return prev - t

    h = _rms_norm(x_ref[...], gain_ref[...])
    dh = minus_previous_token(h, carry_h)
    mu_lora = mu_lora_ref[...]

    def lora_hidden(mu_row, w_first_ref):
        return _dot((h + dh * mu_row).astype(BF16), w_first_ref[...])

    hb = h.astype(BF16)
    rkv = _dot(hb, w_in_ref[:, 3 * D_GROUP:])
    hid_w = lora_hidden(mu_lora[0:1], w1_ref)
    hid_a = lora_hidden(mu_lora[1:2], a1_ref)
    hid_g = lora_hidden(mu_lora[2:3], g1_ref)
    if has_vres:
        hid_v = lora_hidden(mu_v_ref[...], v1_ref)

    qkv = _dot(hb, w_in_ref[:, :3 * D_GROUP])

    def second(hidden, w_second_ref):
        return _dot(hidden.astype(BF16), w_second_ref[...])

    w_log = -_softplus(-(w0_ref[...] + second(jnp.tanh(hid_w), w2_ref))) - 0.5
    lw_out[...] = -jnp.exp(w_log)
    a = jax.nn.sigmoid(a0_ref[...] + second(hid_a, a2_ref))
    a_out[...] = a
    g_out[...] = second(jax.nn.sigmoid(hid_g), g2_ref)

    drkv = minus_previous_token(rkv, carry_p)
    mu_rkv = mu_rkv_ref[...]
    r_out[...] = rkv[:, 0:D_GROUP] + drkv[:, 0:D_GROUP] * mu_rkv[0:1]
    k = rkv[:, D_GROUP:2 * D_GROUP] + drkv[:, D_GROUP:2 * D_GROUP] * mu_rkv[1:2]
    kku_out[...] = k * kk_ref[...]
    km_out[...] = k * (1.0 + (a - 1.0) * ka_ref[...])
    v = rkv[:, 2 * D_GROUP:] + drkv[:, 2 * D_GROUP:] * mu_rkv[2:3]
    if has_vres:
        gate = jax.nn.sigmoid(v0_ref[...] + second(hid_v, v2_ref))
        v = v + (vfirst_ref[...] - v) * gate
    vv_out[...] = v

    q_out[...] = (qkv[:, 0:D_GROUP] * HEAD_DIM ** -0.5).astype(BF16)
    k_out[...] = qkv[:, D_GROUP:2 * D_GROUP].astype(BF16)
    v_out[...] = qkv[:, 2 * D_GROUP:3 * D_GROUP].astype(BF16)


def _mix_in(x2d, layer, seq_len, p, v_first):
    n_tok, d = x2d.shape
    tm = TM_MIX
    has_vres = layer > 0
    tok = lambda width: pl.BlockSpec((tm, width), lambda i: (i, 0))
    ins = [x2d, p["norm_mix"], p["w_in"], p["mu_rkv"], p["mu_lora"], p["w0"], p["w1"], p["w2"],
           p["a0"], p["a1"], p["a2"], p["g1"], p["g2"], p["k_k"], p["k_a"]]
    in_specs = [tok(d)] + [_layer_spec(a, layer) for a in ins[1:]]
    if has_vres:
        extra = [p["mu_v"], p["v0"], p["v1"], p["v2"]]
        ins += extra + [v_first]
        in_specs += [_layer_spec(a, layer - 1) for a in extra] + [tok(D_GROUP)]
    out_shape = ([jax.ShapeDtypeStruct((n_tok, D_GROUP), BF16)] * 3
                 + [jax.ShapeDtypeStruct((n_tok, D_GROUP), F32)] * 7)
    return pl.pallas_call(
        functools.partial(_mix_in_kernel, has_vres=has_vres, tm=tm, tiles_per_seq=seq_len // tm),
        grid=(n_tok // tm,),
        in_specs=in_specs,
        out_specs=[tok(D_GROUP)] * 10,
        out_shape=out_shape,
        scratch_shapes=[pltpu.VMEM((1, d), F32), pltpu.VMEM((1, 3 * D_GROUP), F32)],
        compiler_params=_params(1),
        name="mix_in",
    )(*ins)


def _attn_kernel(q_ref, k_ref, v_ref, gain_ref, o_ref, q_st, kv_st, acc_ref, stat_ref, *,
                 spans_per_seq, patterns):
    blk, span = ATTN_BLOCK, ATTN_SPAN
    j = pl.program_id(0) % spans_per_seq
    cur_base = (j % 2) * span
    prev_base = span - cur_base

    @pl.when(j == 0)
    def _no_previous_span():
        kv_st[:, pl.ds(prev_base, span), :] = jnp.zeros((2 * N_PAIRS, span, PAIR), F32)

    for pair in range(N_PAIRS):
        cols = slice(pair * PAIR, (pair + 1) * PAIR)
        q_st[pair] = q_ref[:, cols].astype(F32)
        kv_st[pair, pl.ds(cur_base, span), :] = k_ref[:, cols].astype(F32)
        kv_st[N_PAIRS + pair, pl.ds(cur_base, span), :] = v_ref[:, cols].astype(F32)

    qi = lax.broadcasted_iota(jnp.int32, (2 * blk, 2 * blk), 0) % blk
    ki = lax.broadcasted_iota(jnp.int32, (2 * blk, 2 * blk), 1)
    dist = qi + blk - ki
    even = lax.broadcasted_iota(jnp.int32, (1, PAIR), 1) < HEAD_DIM
    ones = jnp.ones((2 * blk, PAIR), BF16)
    pairs = range(N_PAIRS)

    def run_pass(n_back, dil, first, last):
        blocks_per_sub = span // (blk * dil)
        shift = blocks_per_sub.bit_length() - 1
        stride = None if dil == 1 else dil

        def rows(start):
            return pl.ds(start, blk, stride=stride) if stride else pl.ds(pl.multiple_of(start, blk), blk)

        def by_head(stacked):
            return jnp.where(even, stacked[:blk], stacked[blk:])

        def blocks_step(step, carry):
            starts, valids, items = [], [], []
            for ib in range(ATTN_BLOCKS_PER_STEP):
                t = step * ATTN_BLOCKS_PER_STEP + ib
                r = lax.shift_right_logical(t, shift)
                n = t & (blocks_per_sub - 1)
                start = n * (blk * dil) + r
                cur = cur_base + start
                prev = jnp.where(n > 0, cur - blk * dil, prev_base + span - blk * dil + r)
                first_key = jnp.where((n > 0) | (j > 0), 0, blk)
                starts.append(start)
                valids.append((dist >= 0) & (dist <= n_back) & (ki >= first_key))
                items += [(ib, p, cur, prev) for p in pairs]

            if not first:
                m_prev = [stat_ref[p, rows(starts[ib]), :] for ib, p, _, _ in items]
                l_prev = [stat_ref[N_PAIRS + p, rows(starts[ib]), :] for ib, p, _, _ in items]
                acc_prev = [acc_ref[p, rows(starts[ib]), :] for ib, p, _, _ in items]
            zero = jnp.zeros((blk, PAIR), BF16)
            work = [dict() for _ in items]

            def scores(i):
                ib, p, cur, prev = items[i]
                q = q_st[p, rows(starts[ib]), :].astype(BF16)
                qq = jnp.concatenate([jnp.where(even, q, zero), jnp.where(even, zero, q)], axis=0)
                key = jnp.concatenate([kv_st[p, rows(prev), :].astype(BF16),
                                       kv_st[p, rows(cur), :].astype(BF16)], axis=0)
                work[i]["s"] = jnp.where(valids[ib], _dot_nt(qq, key), MASK_VALUE)

            def weights(i):
                s = work[i].pop("s")
                m_new = jnp.broadcast_to(jnp.max(jnp.maximum(s[:, :blk], s[:, blk:]), axis=-1, keepdims=True),
                                         (2 * blk, PAIR))
                if not first:
                    swapped = pltpu.roll(m_prev[i], HEAD_DIM, 1)
                    m_old = jnp.concatenate([jnp.where(even, m_prev[i], swapped),
                                             jnp.where(even, swapped, m_prev[i])], axis=0)
                    m_new = jnp.maximum(m_new, m_old)
                    work[i]["alpha"] = by_head(jnp.exp(m_old - m_new))
                work[i]["m"] = by_head(m_new)
                work[i]["p"] = jnp.exp(s - jnp.concatenate([m_new, m_new], axis=1)).astype(BF16)

            def accumulate(i):
                ib, p, cur, prev = items[i]
                val = jnp.concatenate([kv_st[N_PAIRS + p, rows(prev), :].astype(BF16),
                                       kv_st[N_PAIRS + p, rows(cur), :].astype(BF16)], axis=0)
                pv = _dot(work[i].pop("p"), jnp.concatenate([val, ones], axis=1))
                num, l_new = by_head(pv[:, :PAIR]), by_head(pv[:, PAIR:])
                if not first:
                    alpha = work[i].pop("alpha")
                    l_new = alpha * l_prev[i] + l_new
                    num = alpha * acc_prev[i] + num
                if last:
                    acc_ref[p, rows(starts[ib]), :] = num / l_new
                else:
                    acc_ref[p, rows(starts[ib]), :] = num
                    stat_ref[p, rows(starts[ib]), :] = work[i].pop("m")
                    stat_ref[N_PAIRS + p, rows(starts[ib]), :] = l_new

            for stage in (scores, weights, accumulate):
                for i in range(len(items)):
                    stage(i)
            return carry

        lax.fori_loop(0, span // (blk * ATTN_BLOCKS_PER_STEP), blocks_step, 0)

    for idx, (n_back, dil) in enumerate(patterns):
        run_pass(n_back, dil, first=idx == 0, last=idx == len(patterns) - 1)

    def normalise(ic, carry):
        rows = pl.ds(pl.multiple_of(ic * ATTN_NORM_ROWS, ATTN_NORM_ROWS), ATTN_NORM_ROWS)
        out = jnp.concatenate([acc_ref[p, rows, :] for p in pairs], axis=-1)
        o_ref[rows, :] = _rms_norm(out, gain_ref[...]).astype(BF16)
        return carry

    lax.fori_loop(0, span // ATTN_NORM_ROWS, normalise, 0)


def _dilated_attention(q, k, v, gain, layer, seq_len):
    n_tok, width = q.shape
    assert seq_len % ATTN_SPAN == 0, "sequence length must be a multiple of the largest dilated span"
    patterns = tuple(sorted(((w // d, d) for w, d in DILATED_PATTERNS), key=lambda nd: -nd[1]))
    assert patterns[-1][1] == 1 and all(ATTN_SPAN % (ATTN_BLOCK * d) == 0 for _, d in patterns)
    assert all(n_back <= ATTN_BLOCK for n_back, _ in patterns), \
        "a query block may only reach into the previous key block"
    tok = pl.BlockSpec((ATTN_SPAN, width), lambda i: (i, 0))
    tok_in = pl.BlockSpec((ATTN_SPAN, width), lambda i: (i, 0), pipeline_mode=pl.Buffered(1))
    return pl.pallas_call(
        functools.partial(_attn_kernel, spans_per_seq=seq_len // ATTN_SPAN, patterns=patterns),
        grid=(n_tok // ATTN_SPAN,),
        in_specs=[tok_in, tok_in, tok_in, _layer_spec(gain, layer)],
        out_specs=tok,
        out_shape=jax.ShapeDtypeStruct((n_tok, width), BF16),
        scratch_shapes=[pltpu.VMEM((N_PAIRS, ATTN_SPAN, PAIR), F32),
                        pltpu.VMEM((2 * N_PAIRS, 2 * ATTN_SPAN, PAIR), F32),
                        pltpu.VMEM((N_PAIRS, ATTN_SPAN, PAIR), F32),
                        pltpu.VMEM((2 * N_PAIRS, ATTN_SPAN, PAIR), F32)],
        compiler_params=_params(1),
        name="attn",
    )(q, k, v, gain)


def _split3(x):
    hi = x.astype(BF16)
    r1 = x - hi.astype(F32)
    mid = r1.astype(BF16)
    lo = (r1 - mid.astype(F32)).astype(BF16)
    return hi, mid, lo


def _wkv_kernel(r_ref, lw_ref, k_ref, v_ref, kku_ref, a_ref, g_ref, rk_ref, lnw_ref, lnb_ref,
                o_ref, st_ref, *, tw, steps_per_seq):
    @pl.when(pl.program_id(0) % steps_per_seq == 0)
    def _start_of_sequence():
        st_ref[...] = jnp.zeros_like(st_ref)

    c = WKV_CHUNK
    even = lax.broadcasted_iota(jnp.int32, (1, PAIR), 1) < HEAD_DIM
    ri = lax.broadcasted_iota(jnp.int32, (PAIR, PAIR), 0)
    ci = lax.broadcasted_iota(jnp.int32, (PAIR, PAIR), 1)
    same_head = (ri // c) == (ci // c)
    strictly_lower = same_head & ((ri % c) > (ci % c))
    lower = same_head & ((ri % c) >= (ci % c))
    eye =(ri == ci).astype(F32)
    merge_masks = []
    s = 1
    while s < c:
        merge_masks.append(((ri // (2 * s)) == (ci // (2 * s)))
                           & ((ri // s) % 2 == 1) & ((ci // s) % 2 == 0))
        s *= 2

    def stack(t):
        return jnp.concatenate([jnp.where(even, t, 0.0), jnp.where(even, 0.0, t)], axis=0)

    def head_sum(t):
        s_even = jnp.sum(jnp.where(even, t, 0.0), axis=-1, keepdims=True)
        s_odd = jnp.sum(jnp.where(even, 0.0, t), axis=-1, keepdims=True)
        return jnp.where(even, s_even, s_odd)

    def bdot(a, b):
        return _dot(a.astype(BF16), b.astype(BF16))

    n_chunks = tw // c
    items = [(ic, pair) for ic in range(n_chunks) for pair in range(N_PAIRS)]

    def tile(ref_or_val, ic, pair):
        return ref_or_val[ic * c:(ic + 1) * c, pair * PAIR:(pair + 1) * PAIR]

    lw_all = lw_ref[...]
    bi = lax.broadcasted_iota(jnp.int32, (tw, tw), 0)
    bj = lax.broadcasted_iota(jnp.int32, (tw, tw), 1)
    cumsum_mat = (((bi // c) == (bj // c)) & (bi >= bj)).astype(BF16)
    cum_all = sum(_dot(cumsum_mat, piece) for piece in _split3(lw_all))

    prep = []
    for ic, pair in items:
        lw, cum = tile(lw_all, ic, pair), tile(cum_all, ic, pair)
        total = cum[c - 1:, :]
        kku = tile(kku_ref, ic, pair)
        kk = kku / jnp.maximum(jnp.sqrt(head_sum(kku * kku)), 1e-12)
        b_vec = kk * tile(a_ref, ic, pair)
        k_mod = tile(k_ref, ic, pair)
        r = tile(r_ref, ic, pair)
        v = tile(v_ref, ic, pair)
        w_inv = jnp.exp(-cum)
        w_rest = jnp.exp(total - cum)
        r_t = r * jnp.exp(cum)
        la = stack(-kk * jnp.exp(cum - lw))
        v_swapped = pltpu.roll(v, HEAD_DIM, 1)
        prep.append(dict(
            r=r, v=v, k_mod=k_mod, r_t=r_t, la=la, decay=jnp.exp(total),
            lhs=jnp.concatenate([la, stack(r_t)], axis=0).astype(BF16),
            rhs=jnp.concatenate([stack(b_vec * w_inv), stack(k_mod * w_inv)], axis=0).astype(BF16),
            lv_swapped=jnp.concatenate([jnp.where(even, 0.0, v_swapped),
                                        jnp.where(even, v_swapped, 0.0)], axis=0).astype(BF16),
            bk_rest=jnp.concatenate([b_vec * w_rest, k_mod * w_rest], axis=0).astype(BF16)))

    aas = [_dot_nt(it["lhs"], it["rhs"]) for it in prep]
    a_abs = [jnp.where(strictly_lower, aa[:PAIR, :PAIR], 0.0) for aa in aas]
    xs = [it["la"] + _dot(jnp.where(strictly_lower, aa[:PAIR, PAIR:], 0.0).astype(BF16), it["lv_swapped"])
          for it, aa in zip(prep, aas)]
    m_mix = [jnp.concatenate([jnp.where(lower, aa[PAIR:, :PAIR], 0.0),
                              jnp.where(lower, aa[PAIR:, PAIR:], 0.0)], axis=1).astype(BF16) for aa in aas]
    t_invs = [eye + jnp.where(merge_masks[0], a_ab, 0.0) for a_ab in a_abs]
    for mask in merge_masks[1:]:
        t_bf = [t.astype(BF16) for t in t_invs]
        inner = [_dot(jnp.where(mask, a_ab, 0.0).astype(BF16), t) for a_ab, t in zip(a_abs, t_bf)]
        t_invs = [t + _dot(tb, w.astype(BF16)) for t, tb, w in zip(t_invs, t_bf, inner)]
    xs = [bdot(t, x) for t, x in zip(t_invs, xs)]

    out_rows = []
    states = [st_ref[pair] for pair in range(N_PAIRS)]
    for ic in range(n_chunks):
        chunk = [(prep[ic * N_PAIRS + pair], xs[ic * N_PAIRS + pair], m_mix[ic * N_PAIRS + pair])
                 for pair in range(N_PAIRS)]
        p_mats = [jnp.where(even, x[:c], x[c:]) for _, x, _ in chunk]
        q_mats = [pltpu.roll(jnp.where(even, x[c:], x[:c]), HEAD_DIM, 1) for _, x, _ in chunk]
        urs = [_dot_nt(jnp.concatenate([p_mat, it["r_t"]], axis=0).astype(BF16), state.astype(BF16))
               for (it, _, _), p_mat, state in zip(chunk, p_mats, states)]
        us = [ur[:c] + q_mat for ur, q_mat in zip(urs, q_mats)]
        y_mixes = [_dot(mm, jnp.concatenate([stack(u), stack(it["v"])], axis=0).astype(BF16))
                   for (it, _, mm), u in zip(chunk, us)]
        updates = [_dot_tn(jnp.concatenate([u, it["v"]], axis=0).astype(BF16), it["bk_rest"])
                   for (it, _, _), u in zip(chunk, us)]
        states = [state * it["decay"] + jnp.where(same_head, update, 0.0)
                  for (it, _, _), state, update in zip(chunk, states, updates)]
        outs = []
        for pair, ((it, _, _), ur, y_mix) in enumerate(zip(chunk, urs, y_mixes)):
            cols = slice(pair * PAIR, (pair + 1) * PAIR)
            y = ur[c:] + y_mix[:c] + y_mix[c:]
            mean = head_sum(y) * (1.0 / HEAD_DIM)
            yc = y - mean
            var = head_sum(yc * yc) * (1.0 / HEAD_DIM)
            y_norm = yc * lax.rsqrt(var + GN_EPS) * lnw_ref[:, cols] + lnb_ref[:, cols]
            bonus = head_sum(it["r"] * it["k_mod"] * rk_ref[:, cols]) * it["v"]
            outs.append(((y_norm + bonus) * tile(g_ref, ic, pair)).astype(BF16))
        out_rows.append(jnp.concatenate(outs, axis=-1))
    for pair in range(N_PAIRS):
        st_ref[pair] = states[pair]
    o_ref[...] = jnp.concatenate(out_rows, axis=0)


def _wkv(r, lw, k_mod, v, kku, a, g, layer, seq_len, p):
    n_tok = r.shape[0]
    tw = TW_WKV
    tok = pl.BlockSpec((tw, D_GROUP), lambda i: (i, 0))
    vec = lambda arr: _layer_spec(arr, layer)
    return pl.pallas_call(
        functools.partial(_wkv_kernel, tw=tw, steps_per_seq=seq_len // tw),
        grid=(n_tok // tw,),
        in_specs=[tok] * 7 + [vec(p["r_k"]), vec(p["ln_w"]), vec(p["ln_b"])],
        out_specs=tok,
        out_shape=jax.ShapeDtypeStruct((n_tok, D_GROUP), BF16),
        scratch_shapes=[pltpu.VMEM((N_PAIRS, PAIR, PAIR), F32)],
        compiler_params=_params(1),
        name="wkv",
    )(r, lw, k_mod, v, kku, a, g, p["r_k"], p["ln_w"], p["ln_b"])


def _post_kernel(x_ref, attn_ref, yr_ref, wout_ref, gain_ref, wq_ref, kv_ref, wo_ref, o_ref):
    d = x_ref.shape[-1]
    dh = d // N_XATTN_HEADS
    mixed = jnp.concatenate([attn_ref[...], yr_ref[...]], axis=-1)
    x = x_ref[...] + _dot(mixed, wout_ref[...])
    h = _rms_norm(x, gain_ref[...]).astype(BF16)
    q = (_dot(h, wq_ref[...]) * dh ** -0.5).astype(BF16)
    heads = []
    for hd in range(N_XATTN_HEADS):
        s = _dot_nt(q[:, hd * dh:(hd + 1) * dh], kv_ref[:, hd * dh:(hd + 1) * dh])
        e = jnp.exp(s - jnp.max(s, axis=-1, keepdims=True))
        o = _dot(e.astype(BF16), kv_ref[:, d + hd * dh:d + (hd + 1) * dh])
        heads.append((o / jnp.sum(e, axis=-1, keepdims=True)).astype(BF16))
    o_ref[...] = x + _dot(jnp.concatenate(heads, axis=-1), wo_ref[...])


def _post(x2d, attn2d, yr, kv, layer, seq_len, p):
    n_tok, d = x2d.shape
    tm = TM_POST
    tiles_per_seq = seq_len // tm
    mem_len = kv.shape[2]
    tok = lambda width: pl.BlockSpec((tm, width), lambda i: (i, 0))
    return pl.pallas_call(
        _post_kernel,
        grid=(n_tok // tm,),
        in_specs=[tok(d), tok(D_GROUP), tok(D_GROUP), _layer_spec(p["w_out"], layer),
                  _layer_spec(p["norm_xattn"], layer), _layer_spec(p["wq"], layer),
                  pl.BlockSpec((None, None, mem_len, 2 * d), lambda i: (layer, i // tiles_per_seq, 0, 0)),
                  _layer_spec(p["wo"], layer)],
        out_specs=tok(d),
        out_shape=jax.ShapeDtypeStruct((n_tok, d), F32),
        compiler_params=_params(1),
        name="post",
    )(x2d, attn2d, yr, p["w_out"], p["norm_xattn"], p["wq"], kv, p["wo"])


def _ffn_kernel(*refs, final):
    x_ref, gain_ref, wg_ref, wu_ref, wd_ref = refs[:5]
    o_ref = refs[-1]
    x = x_ref[...]
    h = _rms_norm(x, gain_ref[...]).astype(BF16)
    d_ff = wg_ref.shape[-1]
    step = d_ff // FFN_CHUNKS
    y = x
    for ic in range(FFN_CHUNKS):
        cols = slice(ic * step, (ic + 1) * step)
        gate = _dot(h, wg_ref[:, cols])
        up = _dot(h, wu_ref[:, cols])
        act = (gate * jax.nn.sigmoid(gate) * up).astype(BF16)
        y = y + _dot(act, wd_ref[cols, :])
    if final:
        y = _rms_norm(y, refs[5][...])
    o_ref[...] = y


def _ffn(x2d, layer, p, final_gain):
    n_tok, d = x2d.shape
    tm = TM_FFN
    final = final_gain is not None
    tok = pl.BlockSpec((tm, d), lambda i: (i, 0))
    ins = [x2d, p["norm_ffn"], p["w_gate"], p["w_up"], p["w_down"]]
    in_specs = [tok] + [_layer_spec(a, layer) for a in ins[1:]]
    if final:
        ins.append(final_gain)
        in_specs.append(_const_spec(final_gain.shape, (0, 0)))
    return pl.pallas_call(
        functools.partial(_ffn_kernel, final=final),
        grid=(n_tok // tm,),
        in_specs=in_specs,
        out_specs=tok,
        out_shape=jax.ShapeDtypeStruct((n_tok, d), F32),
        compiler_params=_params(1),
        name="ffn",
    )(*ins)


def kernel(x, mem, mem_norm, norm_mix, w_in, attn_out_norm, rwkv_mu_rkv, rwkv_mu_lora, rwkv_mu_v, rwkv_w0, rwkv_w1, rwkv_w2, rwkv_a0, rwkv_a1, rwkv_a2, rwkv_v0, rwkv_v1, rwkv_v2, rwkv_g1, rwkv_g2, rwkv_k_k, rwkv_k_a, rwkv_r_k, rwkv_ln_w, rwkv_ln_b, w_out, norm_xattn, xattn_wq, xattn_wkv, xattn_wo, norm_ffn, ffn_w_gate, ffn_w_up, ffn_w_down, norm_final):
    bsz, seq, d = x.shape
    n_layers = w_in.shape[0]
    assert d == 2 * D_GROUP and seq % max(TM_MIX, TM_POST, TM_FFN, TW_WKV) == 0
    row = lambda t: t.reshape(t.shape[0], 1, -1)
    p = {
        "norm_mix": row(norm_mix), "w_in": w_in.astype(BF16),
        "mu_rkv": rwkv_mu_rkv, "mu_lora": rwkv_mu_lora, "mu_v": row(rwkv_mu_v),
        "w0": row(rwkv_w0), "w1": rwkv_w1.astype(BF16), "w2": rwkv_w2.astype(BF16),
        "a0": row(rwkv_a0), "a1": rwkv_a1.astype(BF16), "a2": rwkv_a2.astype(BF16),
        "v0": row(rwkv_v0), "v1": rwkv_v1.astype(BF16), "v2": rwkv_v2.astype(BF16),
        "g1": rwkv_g1.astype(BF16), "g2": rwkv_g2.astype(BF16),
        "k_k": row(rwkv_k_k), "k_a": row(rwkv_k_a), "r_k": row(rwkv_r_k),
        "ln_w": row(rwkv_ln_w), "ln_b": row(rwkv_ln_b),
        "w_out": w_out.astype(BF16), "norm_xattn": row(norm_xattn),
        "wq": xattn_wq.astype(BF16), "wo": xattn_wo.astype(BF16),
        "norm_ffn": row(norm_ffn), "w_gate": ffn_w_gate.astype(BF16),
        "w_up": ffn_w_up.astype(BF16), "w_down": ffn_w_down.astype(BF16),
    }
    attn_gain = row(attn_out_norm)
    mem_len = mem.shape[1]
    kv = _mem_kv(mem.reshape(bsz * mem_len, d), mem_norm.reshape(1, d), xattn_wkv.astype(BF16))
    kv = kv.reshape(n_layers, bsz, mem_len, 2 * d)

    x2d = x.reshape(bsz * seq, d)
    v_first = None
    for layer in range(n_layers):
        q, k, v, r, lw, k_mod, vv, kku, a, g = _mix_in(x2d, layer, seq, p, v_first)
        if layer == 0:
            v_first = vv
        attn = _dilated_attention(q, k, v, attn_gain, layer, seq)
        yr = _wkv(r, lw, k_mod, vv, kku, a, g, layer, seq, p)
        x2d = _post(x2d, attn, yr, kv, layer, seq, p)
        x2d = _ffn(x2d, layer, p, norm_final.reshape(1, d) if layer == n_layers - 1 else None)
    return x2d.reshape(bsz, seq, d)
```

```python
import functools

import jax
import jax.numpy as jnp
from jax import lax
from jax.experimental import pallas as pl
from jax.experimental.pallas import tpu as pltpu

F32 = jnp.float32
BF16 = jnp.bfloat16

HEAD_DIM = 64
PAIR = 2 * HEAD_DIM
D_GROUP = 512
N_PAIRS = D_GROUP // PAIR
DILATED_PATTERNS = ((128, 1), (512, 4), (2048, 16))
ATTN_BLOCK = 128
ATTN_SPAN = ATTN_BLOCK * max(d for _, d in DILATED_PATTERNS)
ATTN_BLOCKS_PER_STEP = 2
ATTN_NORM_ROWS = 256
N_XATTN_HEADS = 4
NORM_EPS = 1e-6
GN_EPS = 64e-5
MASK_VALUE = -1e30
WKV_CHUNK = 64

V7X_VMEM_LIMIT_BYTES = 56 * 1024 * 1024
TM_MIX = 512
TM_POST = 1024
TM_FFN = 1024
TW_WKV = 256
FFN_CHUNKS = 2


def _dot(a, b):
    return lax.dot_general(a, b, (((1,), (0,)), ((), ())), preferred_element_type=F32)


def _dot_nt(a, b):
    return lax.dot_general(a, b, (((1,), (1,)), ((), ())), preferred_element_type=F32)


def _dot_tn(a, b):
    return lax.dot_general(a, b, (((0,), (0,)), ((), ())), preferred_element_type=F32)


def _rms_norm(x, gain):
    return x * lax.rsqrt(jnp.mean(x * x, axis=-1, keepdims=True) + NORM_EPS) * gain


def _softplus(z):
    return jnp.maximum(z, 0.0) + jnp.log1p(jnp.exp(-jnp.abs(z)))


def _params(n_grid_dims):
    return pltpu.CompilerParams(dimension_semantics=("arbitrary",) * n_grid_dims,
                                vmem_limit_bytes=V7X_VMEM_LIMIT_BYTES)


def _const_spec(shape, index):
    return pl.BlockSpec(shape, lambda *_: index, pipeline_mode=pl.Buffered(1))


def _layer_spec(arr, layer):
    return _const_spec((None,) + arr.shape[1:], (layer,) + (0,) * (arr.ndim - 1))


def _mem_kv_kernel(mem_ref, gain_ref, wkv_ref, o_ref):
    mem_n = _rms_norm(mem_ref[...], gain_ref[...]).astype(BF16)
    o_ref[...] = _dot(mem_n, wkv_ref[...]).astype(BF16)


def _mem_kv(mem2d, gain, wkv):
    n_layers, d, d2 = wkv.shape
    rows = mem2d.shape[0]
    return pl.pallas_call(
        _mem_kv_kernel,
        grid=(n_layers,),
        in_specs=[pl.BlockSpec((rows, d), lambda l: (0, 0)),
                  pl.BlockSpec((1, d), lambda l: (0, 0)),
                  pl.BlockSpec((None, d, d2), lambda l: (l, 0, 0))],
        out_specs=pl.BlockSpec((None, rows, d2), lambda l: (l, 0, 0)),
        out_shape=jax.ShapeDtypeStruct((n_layers, rows, d2), BF16),
        compiler_params=_params(1),
        name="mem_kv",
    )(mem2d, gain, wkv)


def _mix_in_kernel(*refs, has_vres, tm, tiles_per_seq):
    (x_ref, gain_ref, w_in_ref, mu_rkv_ref, mu_lora_ref, w0_ref, w1_ref, w2_ref, a0_ref, a1_ref,
     a2_ref, g1_ref, g2_ref, kk_ref, ka_ref) = refs[:15]
    rest = refs[15:]
    if has_vres:
        mu_v_ref, v0_ref, v1_ref, v2_ref, vfirst_ref = rest[:5]
        rest = rest[5:]
    (q_out, k_out, v_out, r_out, lw_out, km_out, vv_out, kku_out, a_out, g_out,
     carry_h, carry_p) = rest

    @pl.when(pl.program_id(0) % tiles_per_seq == 0)
    def _start_of_sequence():
        carry_h[...] = jnp.zeros_like(carry_h)
        carry_p[...] = jnp.zeros_like(carry_p)

    row = lax.broadcasted_iota(jnp.int32, (tm, 1), 0)

    def minus_previous_token(t, carry_ref):
        prev = jnp.where(row == 0, carry_ref[...], pltpu.roll(t, 1, 0))
        carry_ref[...] = t[tm - 1:, :]
        return prev - t

    h = _rms_norm(x_ref[...], gain_ref[...])
    dh = minus_previous_token(h, carry_h)
    mu_lora = mu_lora_ref[...]

    def lora_hidden(mu_row, w_first_ref):
        return _dot((h + dh * mu_row).astype(BF16), w_first_ref[...])

    hb = h.astype(BF16)
    rkv = _dot(hb, w_in_ref[:, 3 * D_GROUP:])
    hid_w = lora_hidden(mu_lora[0:1], w1_ref)
    hid_a = lora_hidden(mu_lora[1:2], a1_ref)
    hid_g = lora_hidden(mu_lora[2:3], g1_ref)
    if has_vres:
        hid_v = lora_hidden(mu_v_ref[...], v1_ref)

    qkv = _dot(hb, w_in_ref[:, :3 * D_GROUP])

    def second(hidden, w_second_ref):
        return _dot(hidden.astype(BF16), w_second_ref[...])

    w_log = -_softplus(-(w0_ref[...] + second(jnp.tanh(hid_w), w2_ref))) - 0.5
    lw_out[...] = -jnp.exp(w_log)
    a = jax.nn.sigmoid(a0_ref[...] + second(hid_a, a2_ref))
    a_out[...] = a
    g_out[...] = second(jax.nn.sigmoid(hid_g), g2_ref)

    drkv = minus_previous_token(rkv, carry_p)
    mu_rkv = mu_rkv_ref[...]
    r_out[...] = rkv[:, 0:D_GROUP] + drkv[:, 0:D_GROUP] * mu_rkv[0:1]
    k = rkv[:, D_GROUP:2 * D_GROUP] + drkv[:, D_GROUP:2 * D_GROUP] * mu_rkv[1:2]
    kku_out[...] = k * kk_ref[...]
    km_out[...] = k * (1.0 + (a - 1.0) * ka_ref[...])
    v = rkv[:, 2 * D_GROUP:] + drkv[:, 2 * D_GROUP:] * mu_rkv[2:3]
    if has_vres:
        gate = jax.nn.sigmoid(v0_ref[...] + second(hid_v, v2_ref))
        v = v + (vfirst_ref[...] - v) * gate
    vv_out[...] = v

    q_out[...] = (qkv[:, 0:D_GROUP] * HEAD_DIM ** -0.5).astype(BF16)
    k_out[...] = qkv[:, D_GROUP:2 * D_GROUP].astype(BF16)
    v_out[...] = qkv[:, 2 * D_GROUP:3 * D_GROUP].astype(BF16)


def _mix_in(x2d, layer, seq_len, p, v_first):
    n_tok, d = x2d.shape
    tm = TM_MIX
    has_vres = layer > 0
    tok = lambda width: pl.BlockSpec((tm, width), lambda i: (i, 0))
    ins = [x2d, p["norm_mix"], p["w_in"], p["mu_rkv"], p["mu_lora"], p["w0"], p["w1"], p["w2"],
           p["a0"], p["a1"], p["a2"], p["g1"], p["g2"], p["k_k"], p["k_a"]]
    in_specs = [tok(d)] + [_layer_spec(a, layer) for a in ins[1:]]
    if has_vres:
        extra = [p["mu_v"], p["v0"], p["v1"], p["v2"]]
        ins += extra + [v_first]
        in_specs += [_layer_spec(a, layer - 1) for a in extra] + [tok(D_GROUP)]
    out_shape = ([jax.ShapeDtypeStruct((n_tok, D_GROUP), BF16)] * 3
                 + [jax.ShapeDtypeStruct((n_tok, D_GROUP), F32)] * 7)
    return pl.pallas_call(
        functools.partial(_mix_in_kernel, has_vres=has_vres, tm=tm, tiles_per_seq=seq_len // tm),
        grid=(n_tok // tm,),
        in_specs=in_specs,
        out_specs=[tok(D_GROUP)] * 10,
        out_shape=out_shape,
        scratch_shapes=[pltpu.VMEM((1, d), F32), pltpu.VMEM((1, 3 * D_GROUP), F32)],
        compiler_params=_params(1),
        name="mix_in",
    )(*ins)


def _attn_kernel(q_ref, k_ref, v_ref, gain_ref, o_ref, q_st, kv_st, acc_ref, stat_ref, *,
                 spans_per_seq, patterns):
    blk, span = ATTN_BLOCK, ATTN_SPAN
    j = pl.program_id(0) % spans_per_seq
    cur_base = (j % 2) * span
    prev_base = span - cur_base

    @pl.when(j == 0)
    def _no_previous_span():
        kv_st[:, pl.ds(prev_base, span), :] = jnp.zeros((2 * N_PAIRS, span, PAIR), F32)

    for pair in range(N_PAIRS):
        cols = slice(pair * PAIR, (pair + 1) * PAIR)
        q_st[pair] = q_ref[:, cols].astype(F32)
        kv_st[pair, pl.ds(cur_base, span), :] = k_ref[:, cols].astype(F32)
        kv_st[N_PAIRS + pair, pl.ds(cur_base, span), :] = v_ref[:, cols].astype(F32)

    qi = lax.broadcasted_iota(jnp.int32, (2 * blk, 2 * blk), 0) % blk
    ki = lax.broadcasted_iota(jnp.int32, (2 * blk, 2 * blk), 1)
    dist = qi + blk - ki
    even = lax.broadcasted_iota(jnp.int32, (1, PAIR), 1) < HEAD_DIM
    ones = jnp.ones((2 * blk, PAIR), BF16)
    pairs = range(N_PAIRS)

    def run_pass(n_back, dil, first, last):
        blocks_per_sub = span // (blk * dil)
        shift = blocks_per_sub.bit_length() - 1
        stride = None if dil == 1 else dil

        def rows(start):
            return pl.ds(start, blk, stride=stride) if stride else pl.ds(pl.multiple_of(start, blk), blk)

        def by_head(stacked):
            return jnp.where(even, stacked[:blk], stacked[blk:])

        def blocks_step(step, carry):
            starts, valids, items = [], [], []
            for ib in range(ATTN_BLOCKS_PER_STEP):
                t = step * ATTN_BLOCKS_PER_STEP + ib
                r = lax.shift_right_logical(t, shift)
                n = t & (blocks_per_sub - 1)
                start = n * (blk * dil) + r
                cur = cur_base + start
                prev = jnp.where(n > 0, cur - blk * dil, prev_base + span - blk * dil + r)
                first_key = jnp.where((n > 0) | (j > 0), 0, blk)
                starts.append(start)
                valids.append((dist >= 0) & (dist <= n_back) & (ki >= first_key))
                items += [(ib, p, cur, prev) for p in pairs]

            if not first:
                m_prev = [stat_ref[p, rows(starts[ib]), :] for ib, p, _, _ in items]
                l_prev = [stat_ref[N_PAIRS + p, rows(starts[ib]), :] for ib, p, _, _ in items]
                acc_prev = [acc_ref[p, rows(starts[ib]), :] for ib, p, _, _ in items]
            zero = jnp.zeros((blk, PAIR), BF16)
            work = [dict() for _ in items]

            def scores(i):
                ib, p, cur, prev = items[i]
                q = q_st[p, rows(starts[ib]), :].astype(BF16)
                qq = jnp.concatenate([jnp.where(even, q, zero), jnp.where(even, zero, q)], axis=0)
                key = jnp.concatenate([kv_st[p, rows(prev), :].astype(BF16),
                                       kv_st[p, rows(cur), :].astype(BF16)], axis=0)
                work[i]["s"] = jnp.where(valids[ib], _dot_nt(qq, key), MASK_VALUE)

            def weights(i):
                s = work[i].pop("s")
                m_new = jnp.broadcast_to(jnp.max(jnp.maximum(s[:, :blk], s[:, blk:]), axis=-1, keepdims=True),
                                         (2 * blk, PAIR))
                if not first:
                    swapped = pltpu.roll(m_prev[i], HEAD_DIM, 1)
                    m_old = jnp.concatenate([jnp.where(even, m_prev[i], swapped),
                                             jnp.where(even, swapped, m_prev[i])], axis=0)
                    m_new = jnp.maximum(m_new, m_old)
                    work[i]["alpha"] = by_head(jnp.exp(m_old - m_new))
                work[i]["m"] = by_head(m_new)
                work[i]["p"] = jnp.exp(s - jnp.concatenate([m_new, m_new], axis=1)).astype(BF16)

            def accumulate(i):
                ib, p, cur, prev = items[i]
                val = jnp.concatenate([kv_st[N_PAIRS + p, rows(prev), :].astype(BF16),
                                       kv_st[N_PAIRS + p, rows(cur), :].astype(BF16)], axis=0)
                pv = _dot(work[i].pop("p"), jnp.concatenate([val, ones], axis=1))
                num, l_new = by_head(pv[:, :PAIR]), by_head(pv[:, PAIR:])
                if not first:
                    alpha = work[i].pop("alpha")
                    l_new = alpha * l_prev[i] + l_new
                    num = alpha * acc_prev[i] + num
                if last:
                    acc_ref[p, rows(starts[ib]), :] = num / l_new
                else:
                    acc_ref[p, rows(starts[ib]), :] = num
                    stat_ref[p, rows(starts[ib]), :] = work[i].pop("m")
                    stat_ref[N_PAIRS + p, rows(starts[ib]), :] = l_new

            for stage in (scores, weights, accumulate):
                for i in range(len(items)):
                    stage(i)
            return carry

        lax.fori_loop(0, span // (blk * ATTN_BLOCKS_PER_STEP), blocks_step, 0)

    for idx, (n_back, dil) in enumerate(patterns):
        run_pass(n_back, dil, first=idx == 0, last=idx == len(patterns) - 1)

    def normalise(ic, carry):
        rows = pl.ds(pl.multiple_of(ic * ATTN_NORM_ROWS, ATTN_NORM_ROWS), ATTN_NORM_ROWS)
        out = jnp.concatenate([acc_ref[p, rows, :] for p in pairs], axis=-1)
        o_ref[rows, :] = _rms_norm(out, gain_ref[...]).astype(BF16)
        return carry

    lax.fori_loop(0, span // ATTN_NORM_ROWS, normalise, 0)


def _dilated_attention(q, k, v, gain, layer, seq_len):
    n_tok, width = q.shape
    assert seq_len % ATTN_SPAN == 0, "sequence length must be a multiple of the largest dilated span"
    patterns = tuple(sorted(((w // d, d) for w, d in DILATED_PATTERNS), key=lambda nd: -nd[1]))
    assert patterns[-1][1] == 1 and all(ATTN_SPAN % (ATTN_BLOCK * d) == 0 for _, d in patterns)
    assert all(n_back <= ATTN_BLOCK for n_back, _ in patterns), \
        "a query block may only reach into the previous key block"
    tok = pl.BlockSpec((ATTN_SPAN, width), lambda i: (i, 0))
    tok_in = pl.BlockSpec((ATTN_SPAN, width), lambda i: (i, 0), pipeline_mode=pl.Buffered(1))
    return pl.pallas_call(
        functools.partial(_attn_kernel, spans_per_seq=seq_len // ATTN_SPAN, patterns=patterns),
        grid=(n_tok // ATTN_SPAN,),
        in_specs=[tok_in, tok_in, tok_in, _layer_spec(gain, layer)],
        out_specs=tok,
        out_shape=jax.ShapeDtypeStruct((n_tok, width), BF16),
        scratch_shapes=[pltpu.VMEM((N_PAIRS, ATTN_SPAN, PAIR), F32),
                        pltpu.VMEM((2 * N_PAIRS, 2 * ATTN_SPAN, PAIR), F32),
                        pltpu.VMEM((N_PAIRS, ATTN_SPAN, PAIR), F32),
                        pltpu.VMEM((2 * N_PAIRS, ATTN_SPAN, PAIR), F32)],
        compiler_params=_params(1),
        name="attn",
    )(q, k, v, gain)


def _wkv_kernel(r_ref, lw_ref, k_ref, v_ref, kku_ref, a_ref, g_ref, rk_ref, lnw_ref, lnb_ref,
                o_ref, st_ref, *, tw, steps_per_seq):
    @pl.when(pl.program_id(0) % steps_per_seq == 0)
    def _start_of_sequence():
        st_ref[...] = jnp.zeros_like(st_ref)

    c = WKV_CHUNK
    even = lax.broadcasted_iota(jnp.int32, (1, PAIR), 1) < HEAD_DIM
    ri = lax.broadcasted_iota(jnp.int32, (PAIR, PAIR), 0)
    ci = lax.broadcasted_iota(jnp.int32, (PAIR, PAIR), 1)
    same_head = (ri // c) == (ci // c)
    strictly_lower = same_head & ((ri % c) > (ci % c))
    lower = same_head & ((ri % c) >= (ci % c))
    eye =(ri == ci).astype(F32)
    merge_masks = []
    s = 1
    while s < c:
        merge_masks.append(((ri // (2 * s)) == (ci // (2 * s)))
                           & ((ri // s) % 2 == 1) & ((ci // s) % 2 == 0))
        s *= 2

    def stack(t):
        return jnp.concatenate([jnp.where(even, t, 0.0), jnp.where(even, 0.0, t)], axis=0)

    def head_sum(t):
        s_even = jnp.sum(jnp.where(even, t, 0.0), axis=-1, keepdims=True)
        s_odd = jnp.sum(jnp.where(even, 0.0, t), axis=-1, keepdims=True)
        return jnp.where(even, s_even, s_odd)

    def bdot(a, b):
        return _dot(a.astype(BF16), b.astype(BF16))

    n_chunks = tw // c
    items = [(ic, pair) for ic in range(n_chunks) for pair in range(N_PAIRS)]

    def tile(ref_or_val, ic, pair):
        return ref_or_val[ic * c:(ic + 1) * c, pair * PAIR:(pair + 1) * PAIR]

    lw_all = lw_ref[...]
    row_in_chunk = lax.broadcasted_iota(jnp.int32, (tw, 1), 0) % c
    cum_all = lw_all
    shift = 1
    while shift < c:
        cum_all = cum_all + jnp.where(row_in_chunk >= shift, pltpu.roll(cum_all, shift, 0), 0.0)
        shift *= 2

    prep = []
    for ic, pair in items:
        lw, cum = tile(lw_all, ic, pair), tile(cum_all, ic, pair)
        total = cum[c - 1:, :]
        kku = tile(kku_ref, ic, pair)
        kk = kku / jnp.maximum(jnp.sqrt(head_sum(kku * kku)), 1e-12)
        b_vec = kk * tile(a_ref, ic, pair)
        k_mod = tile(k_ref, ic, pair)
        r = tile(r_ref, ic, pair)
        v = tile(v_ref, ic, pair)
        w_inv = jnp.exp(-cum)
        w_rest = jnp.exp(total - cum)
        r_t = r * jnp.exp(cum)
        la = stack(-kk * jnp.exp(cum - lw))
        v_swapped = pltpu.roll(v, HEAD_DIM, 1)
        prep.append(dict(
            r=r, v=v, k_mod=k_mod, r_t=r_t, la=la, decay=jnp.exp(total),
            lhs=jnp.concatenate([la, stack(r_t)], axis=0).astype(BF16),
            rhs=jnp.concatenate([stack(b_vec * w_inv), stack(k_mod * w_inv)], axis=0).astype(BF16),
            lv_swapped=jnp.concatenate([jnp.where(even, 0.0, v_swapped),
                                        jnp.where(even, v_swapped, 0.0)], axis=0).astype(BF16),
            bk_rest=jnp.concatenate([b_vec * w_rest, k_mod * w_rest], axis=0).astype(BF16)))

    aas = [_dot_nt(it["lhs"], it["rhs"]) for it in prep]
    a_abs = [jnp.where(strictly_lower, aa[:PAIR, :PAIR], 0.0) for aa in aas]
    xs = [it["la"] + _dot(jnp.where(strictly_lower, aa[:PAIR, PAIR:], 0.0).astype(BF16), it["lv_swapped"])
          for it, aa in zip(prep, aas)]
    m_mix = [jnp.concatenate([jnp.where(lower, aa[PAIR:, :PAIR], 0.0),
                              jnp.where(lower, aa[PAIR:, PAIR:], 0.0)], axis=1).astype(BF16) for aa in aas]
    t_invs = [eye + jnp.where(merge_masks[0], a_ab, 0.0) for a_ab in a_abs]
    for mask in merge_masks[1:]:
        t_bf = [t.astype(BF16) for t in t_invs]
        inner = [_dot(jnp.where(mask, a_ab, 0.0).astype(BF16), t) for a_ab, t in zip(a_abs, t_bf)]
        t_invs = [t + _dot(tb, w.astype(BF16)) for t, tb, w in zip(t_invs, t_bf, inner)]
    xs = [bdot(t, x) for t, x in zip(t_invs, xs)]

    out_rows = []
    states = [st_ref[pair] for pair in range(N_PAIRS)]
    for ic in range(n_chunks):
        chunk = [(prep[ic * N_PAIRS + pair], xs[ic * N_PAIRS + pair], m_mix[ic * N_PAIRS + pair])
                 for pair in range(N_PAIRS)]
        p_mats = [jnp.where(even, x[:c], x[c:]) for _, x, _ in chunk]
        q_mats = [pltpu.roll(jnp.where(even, x[c:], x[:c]), HEAD_DIM, 1) for _, x, _ in chunk]
        urs = [_dot_nt(jnp.concatenate([p_mat, it["r_t"]], axis=0).astype(BF16), state.astype(BF16))
               for (it, _, _), p_mat, state in zip(chunk, p_mats, states)]
        us = [ur[:c] + q_mat for ur, q_mat in zip(urs, q_mats)]
        y_mixes = [_dot(mm, jnp.concatenate([stack(u), stack(it["v"])], axis=0).astype(BF16))
                   for (it, _, mm), u in zip(chunk, us)]
        updates = [_dot_tn(jnp.concatenate([u, it["v"]], axis=0).astype(BF16), it["bk_rest"])
                   for (it, _, _), u in zip(chunk, us)]
        states = [state * it["decay"] + jnp.where(same_head, update, 0.0)
                  for (it, _, _), state, update in zip(chunk, states, updates)]
        outs = []
        for pair, ((it, _, _), ur, y_mix) in enumerate(zip(chunk, urs, y_mixes)):
            cols = slice(pair * PAIR, (pair + 1) * PAIR)
            y = ur[c:] + y_mix[:c] + y_mix[c:]
            mean = head_sum(y) * (1.0 / HEAD_DIM)
            yc = y - mean
            var = head_sum(yc * yc) * (1.0 / HEAD_DIM)
            y_norm = yc * lax.rsqrt(var + GN_EPS) * lnw_ref[:, cols] + lnb_ref[:, cols]
            bonus = head_sum(it["r"] * it["k_mod"] * rk_ref[:, cols]) * it["v"]
            outs.append(((y_norm + bonus) * tile(g_ref, ic, pair)).astype(BF16))
        out_rows.append(jnp.concatenate(outs, axis=-1))
    for pair in range(N_PAIRS):
        st_ref[pair] = states[pair]
    o_ref[...] = jnp.concatenate(out_rows, axis=0)


def _wkv(r, lw, k_mod, v, kku, a, g, layer, seq_len, p):
    n_tok = r.shape[0]
    tw = TW_WKV
    tok = pl.BlockSpec((tw, D_GROUP), lambda i: (i, 0))
    vec = lambda arr: _layer_spec(arr, layer)
    return pl.pallas_call(
        functools.partial(_wkv_kernel, tw=tw, steps_per_seq=seq_len // tw),
        grid=(n_tok // tw,),
        in_specs=[tok] * 7 + [vec(p["r_k"]), vec(p["ln_w"]), vec(p["ln_b"])],
        out_specs=tok,
        out_shape=jax.ShapeDtypeStruct((n_tok, D_GROUP), BF16),
        scratch_shapes=[pltpu.VMEM((N_PAIRS, PAIR, PAIR), F32)],
        compiler_params=_params(1),
        name="wkv",
    )(r, lw, k_mod, v, kku, a, g, p["r_k"], p["ln_w"], p["ln_b"])


def _post_kernel(x_ref, attn_ref, yr_ref, wout_ref, gain_ref, wq_ref, kv_ref, wo_ref, o_ref):
    d = x_ref.shape[-1]
    dh = d // N_XATTN_HEADS
    mixed = jnp.concatenate([attn_ref[...], yr_ref[...]], axis=-1)
    x = x_ref[...] + _dot(mixed, wout_ref[...])
    h = _rms_norm(x, gain_ref[...]).astype(BF16)
    q = (_dot(h, wq_ref[...]) * dh ** -0.5).astype(BF16)
    heads = []
    for hd in range(N_XATTN_HEADS):
        s = _dot_nt(q[:, hd * dh:(hd + 1) * dh], kv_ref[:, hd * dh:(hd + 1) * dh])
        e = jnp.exp(s - jnp.max(s, axis=-1, keepdims=True))
        o = _dot(e.astype(BF16), kv_ref[:, d + hd * dh:d + (hd + 1) * dh])
        heads.append((o / jnp.sum(e, axis=-1, keepdims=True)).astype(BF16))
    o_ref[...] = x + _dot(jnp.concatenate(heads, axis=-1), wo_ref[...])


def _post(x2d, attn2d, yr, kv, layer, seq_len, p):
    n_tok, d = x2d.shape
    tm = TM_POST
    tiles_per_seq = seq_len // tm
    mem_len = kv.shape[2]
    tok = lambda width: pl.BlockSpec((tm, width), lambda i: (i, 0))
    return pl.pallas_call(
        _post_kernel,
        grid=(n_tok // tm,),
        in_specs=[tok(d), tok(D_GROUP), tok(D_GROUP), _layer_spec(p["w_out"], layer),
                  _layer_spec(p["norm_xattn"], layer), _layer_spec(p["wq"], layer),
                  pl.BlockSpec((None, None, mem_len, 2 * d), lambda i: (layer, i // tiles_per_seq, 0, 0)),
                  _layer_spec(p["wo"], layer)],
        out_specs=tok(d),
        out_shape=jax.ShapeDtypeStruct((n_tok, d), F32),
        compiler_params=_params(1),
        name="post",
    )(x2d, attn2d, yr, p["w_out"], p["norm_xattn"], p["wq"], kv, p["wo"])


def _ffn_kernel(*refs, final):
    x_ref, gain_ref, wg_ref, wu_ref, wd_ref = refs[:5]
    o_ref = refs[-1]
    x = x_ref[...]
    h = _rms_norm(x, gain_ref[...]).astype(BF16)
    d_ff = wg_ref.shape[-1]
    step = d_ff // FFN_CHUNKS
    y = x
    for ic in range(FFN_CHUNKS):
        cols = slice(ic * step, (ic + 1) * step)
        gate = _dot(h, wg_ref[:, cols])
        up = _dot(h, wu_ref[:, cols])
        act = (gate * jax.nn.sigmoid(gate) * up).astype(BF16)
        y = y + _dot(act, wd_ref[cols, :])
    if final:
        y = _rms_norm(y, refs[5][...])
    o_ref[...] = y


def _ffn(x2d, layer, p, final_gain):
    n_tok, d = x2d.shape
    tm = TM_FFN
    final = final_gain is not None
    tok = pl.BlockSpec((tm, d), lambda i: (i, 0))
    ins = [x2d, p["norm_ffn"], p["w_gate"], p["w_up"], p["w_down"]]
    in_specs = [tok] + [_layer_spec(a, layer) for a in ins[1:]]
    if final:
        ins.append(final_gain)
        in_specs.append(_const_spec(final_gain.shape, (0, 0)))
    return pl.pallas_call(
        functools.partial(_ffn_kernel, final=final),
        grid=(n_tok // tm,),
        in_specs=in_specs,
        out_specs=tok,
        out_shape=jax.ShapeDtypeStruct((n_tok, d), F32),
        compiler_params=_params(1),
        name="ffn",
    )(*ins)


def kernel(x, mem, mem_norm, norm_mix, w_in, attn_out_norm, rwkv_mu_rkv, rwkv_mu_lora, rwkv_mu_v, rwkv_w0, rwkv_w1, rwkv_w2, rwkv_a0, rwkv_a1, rwkv_a2, rwkv_v0, rwkv_v1, rwkv_v2, rwkv_g1, rwkv_g2, rwkv_k_k, rwkv_k_a, rwkv_r_k, rwkv_ln_w, rwkv_ln_b, w_out, norm_xattn, xattn_wq, xattn_wkv, xattn_wo, norm_ffn, ffn_w_gate, ffn_w_up, ffn_w_down, norm_final):
    bsz, seq, d = x.shape
    n_layers = w_in.shape[0]
    assert d == 2 * D_GROUP and seq % max(TM_MIX, TM_POST, TM_FFN, TW_WKV) == 0
    row = lambda t: t.reshape(t.shape[0], 1, -1)
    p = {
        "norm_mix": row(norm_mix), "w_in": w_in.astype(BF16),
        "mu_rkv": rwkv_mu_rkv, "mu_lora": rwkv_mu_lora, "mu_v": row(rwkv_mu_v),
        "w0": row(rwkv_w0), "w1": rwkv_w1.astype(BF16), "w2": rwkv_w2.astype(BF16),
        "a0": row(rwkv_a0), "a1": rwkv_a1.astype(BF16), "a2": rwkv_a2.astype(BF16),
        "v0": row(rwkv_v0), "v1": rwkv_v1.astype(BF16), "v2": rwkv_v2.astype(BF16),
        "g1": rwkv_g1.astype(BF16), "g2": rwkv_g2.astype(BF16),
        "k_k": row(rwkv_k_k), "k_a": row(rwkv_k_a), "r_k": row(rwkv_r_k),
        "ln_w": row(rwkv_ln_w), "ln_b": row(rwkv_ln_b),
        "w_out": w_out.astype(BF16), "norm_xattn": row(norm_xattn),
        "wq": xattn_wq.astype(BF16), "wo": xattn_wo.astype(BF16),
        "norm_ffn": row(norm_ffn), "w_gate": ffn_w_gate.astype(BF16),
        "w_up": ffn_w_up.astype(BF16), "w_down": ffn_w_down.astype(BF16),
    }
    attn_gain = row(attn_out_norm)
    mem_len = mem.shape[1]
    kv = _mem_kv(mem.reshape(bsz * mem_len, d), mem_norm.reshape(1, d), xattn_wkv.astype(BF16))
    kv = kv.reshape(n_layers, bsz, mem_len, 2 * d)

    x2d = x.reshape(bsz * seq, d)
    v_first = None
    for layer in range(n_layers):
        q, k, v, r, lw, k_mod, vv, kku, a, g = _mix_in(x2d, layer, seq, p, v_first)
        if layer == 0:
            v_first = vv
        attn = _dilated_attention(q, k, v, attn_gain, layer, seq)
        yr = _wkv(r, lw, k_mod, vv, kku, a, g, layer, seq, p)
        x2d = _post(x2d, attn, yr, kv, layer, seq, p)
        x2d = _ffn(x2d, layer, p, norm_final.reshape(1, d) if layer == n_layers - 1 else None)
    return x2d.reshape(bsz, seq, d)
```

```python
import functools

import jax
import jax.numpy as jnp
from jax import lax
from jax.experimental import pallas as pl
from jax.experimental.pallas import tpu as pltpu

F32 = jnp.float32
BF16 = jnp.bfloat16

HEAD_DIM = 64
PAIR = 2 * HEAD_DIM
D_GROUP = 512
N_PAIRS = D_GROUP // PAIR
DILATED_PATTERNS = ((128, 1), (512, 4), (2048, 16))
ATTN_BLOCK = 128
ATTN_SPAN = ATTN_BLOCK * max(d for _, d in DILATED_PATTERNS)
ATTN_BLOCKS_PER_STEP = 2
ATTN_NORM_ROWS = 256
N_XATTN_HEADS = 4
NORM_EPS = 1e-6
GN_EPS = 64e-5
MASK_VALUE = -1e30
WKV_CHUNK = 64

V7X_VMEM_LIMIT_BYTES = 56 * 1024 * 1024
TM_MIX = 512
TM_POST = 1024
TM_FFN = 1024
TW_WKV = 256
FFN_CHUNKS = 2


def _dot(a, b):
    return lax.dot_general(a, b, (((1,), (0,)), ((), ())), preferred_element_type=F32)


def _dot_nt(a, b):
    return lax.dot_general(a, b, (((1,), (1,)), ((), ())), preferred_element_type=F32)


def _dot_tn(a, b):
    return lax.dot_general(a, b, (((0,), (0,)), ((), ())), preferred_element_type=F32)


def _rms_norm(x, gain):
    return x * lax.rsqrt(jnp.mean(x * x, axis=-1, keepdims=True) + NORM_EPS) * gain


def _softplus(z):
    return jnp.maximum(z, 0.0) + jnp.log1p(jnp.exp(-jnp.abs(z)))


def _params(n_grid_dims):
    return pltpu.CompilerParams(dimension_semantics=("arbitrary",) * n_grid_dims,
                                vmem_limit_bytes=V7X_VMEM_LIMIT_BYTES)


def _const_spec(shape, index):
    return pl.BlockSpec(shape, lambda *_: index, pipeline_mode=pl.Buffered(1))


def _layer_spec(arr, layer):
    return _const_spec((None,) + arr.shape[1:], (layer,) + (0,) * (arr.ndim - 1))


def _mem_kv_kernel(mem_ref, gain_ref, wkv_ref, o_ref):
    mem_n = _rms_norm(mem_ref[...], gain_ref[...]).astype(BF16)
    o_ref[...] = _dot(mem_n, wkv_ref[...]).astype(BF16)


def _mem_kv(mem2d, gain, wkv):
    n_layers, d, d2 = wkv.shape
    rows = mem2d.shape[0]
    return pl.pallas_call(
        _mem_kv_kernel,
        grid=(n_layers,),
        in_specs=[pl.BlockSpec((rows, d), lambda l: (0, 0)),
                  pl.BlockSpec((1, d), lambda l: (0, 0)),
                  pl.BlockSpec((None, d, d2), lambda l: (l, 0, 0))],
        out_specs=pl.BlockSpec((None, rows, d2), lambda l: (l, 0, 0)),
        out_shape=jax.ShapeDtypeStruct((n_layers, rows, d2), BF16),
        compiler_params=_params(1),
        name="mem_kv",
    )(mem2d, gain, wkv)


def _mix_in_kernel(*refs, has_vres, tm, tiles_per_seq):
    (x_ref, gain_ref, w_in_ref, mu_rkv_ref, mu_lora_ref, w0_ref, w1_ref, w2_ref, a0_ref, a1_ref,
     a2_ref, g1_ref, g2_ref, kk_ref, ka_ref) = refs[:15]
    rest = refs[15:]
    if has_vres:
        mu_v_ref, v0_ref, v1_ref, v2_ref, vfirst_ref = rest[:5]
        rest = rest[5:]
    (q_out, k_out, v_out, r_out, lw_out, km_out, vv_out, kku_out, a_out, g_out,
     carry_h, carry_p) = rest

    @pl.when(pl.program_id(0) % tiles_per_seq == 0)
    def _start_of_sequence():
        carry_h[...] = jnp.zeros_like(carry_h)
        carry_p[...] = jnp.zeros_like(carry_p)

    row = lax.broadcasted_iota(jnp.int32, (tm, 1), 0)

    def minus_previous_token(t, carry_ref):
        prev = jnp.where(row == 0, carry_ref[...], pltpu.roll(t, 1, 0))
        carry_ref[...] = t[tm - 1:, :]
        return prev - t

    h = _rms_norm(x_ref[...], gain_ref[...])
    dh = minus_previous_token(h, carry_h)
    mu_lora = mu_lora_ref[...]

    def lora_hidden(mu_row, w_first_ref):
        return _dot((h + dh * mu_row).astype(BF16), w_first_ref[...])

    hb = h.astype(BF16)
    rkv = _dot(hb, w_in_ref[:, 3 * D_GROUP:])
    hid_w = lora_hidden(mu_lora[0:1], w1_ref)
    hid_a = lora_hidden(mu_lora[1:2], a1_ref)
    hid_g = lora_hidden(mu_lora[2:3], g1_ref)
    if has_vres:
        hid_v = lora_hidden(mu_v_ref[...], v1_ref)

    qkv = _dot(hb, w_in_ref[:, :3 * D_GROUP])

    def second(hidden, w_second_ref):
        return _dot(hidden.astype(BF16), w_second_ref[...])

    w_log = -_softplus(-(w0_ref[...] + second(jnp.tanh(hid_w), w2_ref))) - 0.5
    lw_out[...] = -jnp.exp(w_log)
    a = jax.nn.sigmoid(a0_ref[...] + second(hid_a, a2_ref))
    a_out[...] = a
    g_out[...] = second(jax.nn.sigmoid(hid_g), g2_ref)

    drkv = minus_previous_token(rkv, carry_p)
    mu_rkv = mu_rkv_ref[...]
    r_out[...] = rkv[:, 0:D_GROUP] + drkv[:, 0:D_GROUP] * mu_rkv[0:1]
    k = rkv[:, D_GROUP:2 * D_GROUP] + drkv[:, D_GROUP:2 * D_GROUP] * mu_rkv[1:2]
    kku_out[...] = k * kk_ref[...]
    km_out[...] = k * (1.0 + (a - 1.0) * ka_ref[...])
    v = rkv[:, 2 * D_GROUP:] + drkv[:, 2 * D_GROUP:] * mu_rkv[2:3]
    if has_vres:
        gate = jax.nn.sigmoid(v0_ref[...] + second(hid_v, v2_ref))
        v = v + (vfirst_ref[...] - v) * gate
    vv_out[...] = v

    q_out[...] = (qkv[:, 0:D_GROUP] * HEAD_DIM ** -0.5).astype(BF16)
    k_out[...] = qkv[:, D_GROUP:2 * D_GROUP].astype(BF16)
    v_out[...] = qkv[:, 2 * D_GROUP:3 * D_GROUP].astype(BF16)


def _mix_in(x2d, layer, seq_len, p, v_first):
    n_tok, d = x2d.shape
    tm = TM_MIX
    has_vres = layer > 0
    tok = lambda width: pl.BlockSpec((tm, width), lambda i: (i, 0))
    ins = [x2d, p["norm_mix"], p["w_in"], p["mu_rkv"], p["mu_lora"], p["w0"], p["w1"], p["w2"],
           p["a0"], p["a1"], p["a2"], p["g1"], p["g2"], p["k_k"], p["k_a"]]
    in_specs = [tok(d)] + [_layer_spec(a, layer) for a in ins[1:]]
    if has_vres:
        extra = [p["mu_v"], p["v0"], p["v1"], p["v2"]]
        ins += extra + [v_first]
        in_specs += [_layer_spec(a, layer - 1) for a in extra] + [tok(D_GROUP)]
    out_shape = ([jax.ShapeDtypeStruct((n_tok, D_GROUP), BF16)] * 3
                 + [jax.ShapeDtypeStruct((n_tok, D_GROUP), F32)] * 7)
    return pl.pallas_call(
        functools.partial(_mix_in_kernel, has_vres=has_vres, tm=tm, tiles_per_seq=seq_len // tm),
        grid=(n_tok // tm,),
        in_specs=in_specs,
        out_specs=[tok(D_GROUP)] * 10,
        out_shape=out_shape,
        scratch_shapes=[pltpu.VMEM((1, d), F32), pltpu.VMEM((1, 3 * D_GROUP), F32)],
        compiler_params=_params(1),
        name="mix_in",
    )(*ins)


def _attn_kernel(q_ref, k_ref, v_ref, gain_ref, o_ref, q_st, kv_st, acc_ref, stat_ref, prev_kv, *,
                 spans_per_seq, patterns):
    blk, span = ATTN_BLOCK, ATTN_SPAN
    j = pl.program_id(0) % spans_per_seq
    cur_base = (j % 2) * span
    prev_base = span - cur_base

    @pl.when(j == 0)
    def _no_previous_span():
        kv_st[:, pl.ds(prev_base, span), :] = jnp.zeros((2 * N_PAIRS, span, PAIR), F32)
        prev_kv[...] = jnp.zeros_like(prev_kv)

    for pair in range(N_PAIRS):
        cols = slice(pair * PAIR, (pair + 1) * PAIR)
        q_st[pair] = q_ref[:, cols].astype(F32)
        kv_st[pair, pl.ds(cur_base, span), :] = k_ref[:, cols].astype(F32)
        kv_st[N_PAIRS + pair, pl.ds(cur_base, span), :] = v_ref[:, cols].astype(F32)

    qi = lax.broadcasted_iota(jnp.int32, (2 * blk, 2 * blk), 0) % blk
    ki = lax.broadcasted_iota(jnp.int32, (2 * blk, 2 * blk), 1)
    dist = qi + blk - ki
    even = lax.broadcasted_iota(jnp.int32, (1, PAIR), 1) < HEAD_DIM
    ones = jnp.ones((2 * blk, PAIR), BF16)
    pairs = range(N_PAIRS)

    def run_pass(n_back, dil, first, last):
        blocks_per_sub = span // (blk * dil)
        shift = blocks_per_sub.bit_length() - 1
        stride = None if dil == 1 else dil
        previous_block_in_previous_span = blocks_per_sub == 1
        blocks_per_step = ATTN_BLOCKS_PER_STEP

        def rows(start):
            return pl.ds(start, blk, stride=stride) if stride else pl.ds(pl.multiple_of(start, blk), blk)

        def by_head(stacked):
            return jnp.where(even, stacked[:blk], stacked[blk:])

        def blocks_step(step, carry):
            starts, valids, items = [], [], []
            for ib in range(blocks_per_step):
                t = step * blocks_per_step + ib
                r = lax.shift_right_logical(t, shift)
                n = t & (blocks_per_sub - 1)
                start = n * (blk * dil) + r
                cur = cur_base + start
                prev = jnp.where(n > 0, cur - blk * dil, prev_base + span - blk * dil + r)
                first_key = jnp.where((n > 0) | (j > 0), 0, blk)
                starts.append(start)
                valids.append((dist >= 0) & (dist <= n_back) & (ki >= first_key))
                items += [(ib, p, cur, prev, r) for p in pairs]

            if not first:
                m_prev = [stat_ref[p, rows(starts[ib]), :] for ib, p, _, _, _ in items]
                l_prev = [stat_ref[N_PAIRS + p, rows(starts[ib]), :] for ib, p, _, _, _ in items]
                acc_prev = [acc_ref[p, rows(starts[ib]), :] for ib, p, _, _, _ in items]
            zero = jnp.zeros((blk, PAIR), BF16)
            work = [dict() for _ in items]

            def keys_or_values(i, which):
                ib, p, cur, prev, r = items[i]
                cols = slice(p * PAIR, (p + 1) * PAIR)
                slab = which * N_PAIRS + p
                cur_part = kv_st[slab, rows(cur), :].astype(BF16)
                if previous_block_in_previous_span:
                    prev_part = prev_kv[which, r, :, cols]
                    prev_kv[which, r, :, cols] = cur_part
                else:
                    prev_part = kv_st[slab, rows(prev), :].astype(BF16)
                return jnp.concatenate([prev_part, cur_part], axis=0)

            def scores(i):
                ib, p = items[i][:2]
                q = q_st[p, rows(starts[ib]), :].astype(BF16)
                qq = jnp.concatenate([jnp.where(even, q, zero), jnp.where(even, zero, q)], axis=0)
                work[i]["s"] = jnp.where(valids[ib], _dot_nt(qq, keys_or_values(i, 0)), MASK_VALUE)

            def weights(i):
                s = work[i].pop("s")
                m_new = jnp.broadcast_to(jnp.max(jnp.maximum(s[:, :blk], s[:, blk:]), axis=-1, keepdims=True),
                                         (2 * blk, PAIR))
                if not first:
                    swapped = pltpu.roll(m_prev[i], HEAD_DIM, 1)
                    m_old = jnp.concatenate([jnp.where(even, m_prev[i], swapped),
                                             jnp.where(even, swapped, m_prev[i])], axis=0)
                    m_new = jnp.maximum(m_new, m_old)
                    work[i]["alpha"] = by_head(jnp.exp(m_old - m_new))
                work[i]["m"] = by_head(m_new)
                work[i]["p"] = jnp.exp(s - jnp.concatenate([m_new, m_new], axis=1)).astype(BF16)

            def accumulate(i):
                ib, p = items[i][:2]
                pv = _dot(work[i].pop("p"), jnp.concatenate([keys_or_values(i, 1), ones], axis=1))
                num, l_new = by_head(pv[:, :PAIR]), by_head(pv[:, PAIR:])
                if not first:
                    alpha = work[i].pop("alpha")
                    l_new = alpha * l_prev[i] + l_new
                    num = alpha * acc_prev[i] + num
                if last:
                    acc_ref[p, rows(starts[ib]), :] = num / l_new
                else:
                    acc_ref[p, rows(starts[ib]), :] = num
                    stat_ref[p, rows(starts[ib]), :] = work[i].pop("m")
                    stat_ref[N_PAIRS + p, rows(starts[ib]), :] = l_new

            for stage in (scores, weights, accumulate):
                for i in range(len(items)):
                    stage(i)
            return carry

        lax.fori_loop(0, span // (blk * blocks_per_step), blocks_step, 0)

    for idx, (n_back, dil) in enumerate(patterns):
        run_pass(n_back, dil, first=idx == 0, last=idx == len(patterns) - 1)

    def normalise(ic, carry):
        rows = pl.ds(pl.multiple_of(ic * ATTN_NORM_ROWS, ATTN_NORM_ROWS), ATTN_NORM_ROWS)
        out = jnp.concatenate([acc_ref[p, rows, :] for p in pairs], axis=-1)
        o_ref[rows, :] = _rms_norm(out, gain_ref[...]).astype(BF16)
        return carry

    lax.fori_loop(0, span // ATTN_NORM_ROWS, normalise, 0)


def _dilated_attention(q, k, v, gain, layer, seq_len):
    n_tok, width = q.shape
    assert seq_len % ATTN_SPAN == 0, "sequence length must be a multiple of the largest dilated span"
    patterns = tuple(sorted(((w // d, d) for w, d in DILATED_PATTERNS), key=lambda nd: -nd[1]))
    assert patterns[-1][1] == 1 and all(ATTN_SPAN % (ATTN_BLOCK * d) == 0 for _, d in patterns)
    assert all(n_back <= ATTN_BLOCK for n_back, _ in patterns), \
        "a query block may only reach into the previous key block"
    tok = pl.BlockSpec((ATTN_SPAN, width), lambda i: (i, 0))
    tok_in = pl.BlockSpec((ATTN_SPAN, width), lambda i: (i, 0), pipeline_mode=pl.Buffered(1))
    return pl.pallas_call(
        functools.partial(_attn_kernel, spans_per_seq=seq_len // ATTN_SPAN, patterns=patterns),
        grid=(n_tok // ATTN_SPAN,),
        in_specs=[tok_in, tok_in, tok_in, _layer_spec(gain, layer)],
        out_specs=tok,
        out_shape=jax.ShapeDtypeStruct((n_tok, width), BF16),
        scratch_shapes=[pltpu.VMEM((N_PAIRS, ATTN_SPAN, PAIR), F32),
                        pltpu.VMEM((2 * N_PAIRS, 2 * ATTN_SPAN, PAIR), F32),
                        pltpu.VMEM((N_PAIRS, ATTN_SPAN, PAIR), F32),
                        pltpu.VMEM((2 * N_PAIRS, ATTN_SPAN, PAIR), F32),
                        pltpu.VMEM((2, ATTN_SPAN // ATTN_BLOCK, ATTN_BLOCK, width), BF16)],
        compiler_params=_params(1),
        name="attn",
    )(q, k, v, gain)


def _wkv_kernel(r_ref, lw_ref, k_ref, v_ref, kku_ref, a_ref, g_ref, rk_ref, lnw_ref, lnb_ref,
                o_ref, st_ref, *, tw, steps_per_seq):
    @pl.when(pl.program_id(0) % steps_per_seq == 0)
    def _start_of_sequence():
        st_ref[...] = jnp.zeros_like(st_ref)

    c = WKV_CHUNK
    even = lax.broadcasted_iota(jnp.int32, (1, PAIR), 1) < HEAD_DIM
    ri = lax.broadcasted_iota(jnp.int32, (PAIR, PAIR), 0)
    ci = lax.broadcasted_iota(jnp.int32, (PAIR, PAIR), 1)
    same_head = (ri // c) == (ci // c)
    strictly_lower = same_head & ((ri % c) > (ci % c))
    lower = same_head & ((ri % c) >= (ci % c))
    eye =(ri == ci).astype(F32)
    merge_masks = []
    s = 1
    while s < c:
        merge_masks.append(((ri // (2 * s)) == (ci // (2 * s)))
                           & ((ri // s) % 2 == 1) & ((ci // s) % 2 == 0))
        s *= 2

    def stack(t):
        return jnp.concatenate([jnp.where(even, t, 0.0), jnp.where(even, 0.0, t)], axis=0)

    def head_sum(t):
        s_even = jnp.sum(jnp.where(even, t, 0.0), axis=-1, keepdims=True)
        s_odd = jnp.sum(jnp.where(even, 0.0, t), axis=-1, keepdims=True)
        return jnp.where(even, s_even, s_odd)

    def bdot(a, b):
        return _dot(a.astype(BF16), b.astype(BF16))

    n_chunks = tw // c
    items = [(ic, pair) for ic in range(n_chunks) for pair in range(N_PAIRS)]

    def tile(ref_or_val, ic, pair):
        return ref_or_val[ic * c:(ic + 1) * c, pair * PAIR:(pair + 1) * PAIR]

    lw_all = lw_ref[...]
    row_in_chunk = lax.broadcasted_iota(jnp.int32, (tw, 1), 0) % c
    cum_all = lw_all
    shift = 1
    while shift < c:
        cum_all = cum_all + jnp.where(row_in_chunk >= shift, pltpu.roll(cum_all, shift, 0), 0.0)
        shift *= 2

    prep = []
    for ic, pair in items:
        lw, cum = tile(lw_all, ic, pair), tile(cum_all, ic, pair)
        total = cum[c - 1:, :]
        kku = tile(kku_ref, ic, pair)
        kk = kku / jnp.maximum(jnp.sqrt(head_sum(kku * kku)), 1e-12)
        b_vec = kk * tile(a_ref, ic, pair)
        k_mod = tile(k_ref, ic, pair)
        r = tile(r_ref, ic, pair)
        v = tile(v_ref, ic, pair)
        w_inv = jnp.exp(-cum)
        w_rest = jnp.exp(total - cum)
        r_t = r * jnp.exp(cum)
        la = stack(-kk * jnp.exp(cum - lw))
        v_swapped = pltpu.roll(v, HEAD_DIM, 1)
        prep.append(dict(
            r=r, v=v, k_mod=k_mod, r_t=r_t, la=la, decay=jnp.exp(total),
            lhs=jnp.concatenate([la, stack(r_t)], axis=0).astype(BF16),
            rhs=jnp.concatenate([stack(b_vec * w_inv), stack(k_mod * w_inv)], axis=0).astype(BF16),
            lv_swapped=jnp.concatenate([jnp.where(even, 0.0, v_swapped),
                                        jnp.where(even, v_swapped, 0.0)], axis=0).astype(BF16),
            bk_rest=jnp.concatenate([b_vec * w_rest, k_mod * w_rest], axis=0).astype(BF16)))

    aas = [_dot_nt(it["lhs"], it["rhs"]) for it in prep]
    a_abs = [jnp.where(strictly_lower, aa[:PAIR, :PAIR], 0.0) for aa in aas]
    xs = [it["la"] + _dot(jnp.where(strictly_lower, aa[:PAIR, PAIR:], 0.0).astype(BF16), it["lv_swapped"])
          for it, aa in zip(prep, aas)]
    m_mix = [jnp.concatenate([jnp.where(lower, aa[PAIR:, :PAIR], 0.0),
                              jnp.where(lower, aa[PAIR:, PAIR:], 0.0)], axis=1).astype(BF16) for aa in aas]
    t_invs = [eye + jnp.where(merge_masks[0], a_ab, 0.0) for a_ab in a_abs]
    for mask in merge_masks[1:]:
        t_bf = [t.astype(BF16) for t in t_invs]
        inner = [_dot(jnp.where(mask, a_ab, 0.0).astype(BF16), t) for a_ab, t in zip(a_abs, t_bf)]
        t_invs = [t + _dot(tb, w.astype(BF16)) for t, tb, w in zip(t_invs, t_bf, inner)]
    xs = [bdot(t, x) for t, x in zip(t_invs, xs)]

    out_rows = []
    states = [st_ref[pair] for pair in range(N_PAIRS)]
    for ic in range(n_chunks):
        chunk = [(prep[ic * N_PAIRS + pair], xs[ic * N_PAIRS + pair], m_mix[ic * N_PAIRS + pair])
                 for pair in range(N_PAIRS)]
        p_mats = [jnp.where(even, x[:c], x[c:]) for _, x, _ in chunk]
        q_mats = [pltpu.roll(jnp.where(even, x[c:], x[:c]), HEAD_DIM, 1) for _, x, _ in chunk]
        urs = [_dot_nt(jnp.concatenate([p_mat, it["r_t"]], axis=0).astype(BF16), state.astype(BF16))
               for (it, _, _), p_mat, state in zip(chunk, p_mats, states)]
        us = [ur[:c] + q_mat for ur, q_mat in zip(urs, q_mats)]
        y_mixes = [_dot(mm, jnp.concatenate([stack(u), stack(it["v"])], axis=0).astype(BF16))
                   for (it, _, mm), u in zip(chunk, us)]
        updates = [_dot_tn(jnp.concatenate([u, it["v"]], axis=0).astype(BF16), it["bk_rest"])
                   for (it, _, _), u in zip(chunk, us)]
        states = [state * it["decay"] + jnp.where(same_head, update, 0.0)
                  for (it, _, _), state, update in zip(chunk, states, updates)]
        outs = []
        for pair, ((it, _, _), ur, y_mix) in enumerate(zip(chunk, urs, y_mixes)):
            cols = slice(pair * PAIR, (pair + 1) * PAIR)
            y = ur[c:] + y_mix[:c] + y_mix[c:]
            mean = head_sum(y) * (1.0 / HEAD_DIM)
            yc = y - mean
            var = head_sum(yc * yc) * (1.0 / HEAD_DIM)
            y_norm = yc * lax.rsqrt(var + GN_EPS) * lnw_ref[:, cols] + lnb_ref[:, cols]
            bonus = head_sum(it["r"] * it["k_mod"] * rk_ref[:, cols]) * it["v"]
            outs.append(((y_norm + bonus) * tile(g_ref, ic, pair)).astype(BF16))
        out_rows.append(jnp.concatenate(outs, axis=-1))
    for pair in range(N_PAIRS):
        st_ref[pair] = states[pair]
    o_ref[...] = jnp.concatenate(out_rows, axis=0)


def _wkv(r, lw, k_mod, v, kku, a, g, layer, seq_len, p):
    n_tok = r.shape[0]
    tw = TW_WKV
    tok = pl.BlockSpec((tw, D_GROUP), lambda i: (i, 0))
    vec = lambda arr: _layer_spec(arr, layer)
    return pl.pallas_call(
        functools.partial(_wkv_kernel, tw=tw, steps_per_seq=seq_len // tw),
        grid=(n_tok // tw,),
        in_specs=[tok] * 7 + [vec(p["r_k"]), vec(p["ln_w"]), vec(p["ln_b"])],
        out_specs=tok,
        out_shape=jax.ShapeDtypeStruct((n_tok, D_GROUP), BF16),
        scratch_shapes=[pltpu.VMEM((N_PAIRS, PAIR, PAIR), F32)],
        compiler_params=_params(1),
        name="wkv",
    )(r, lw, k_mod, v, kku, a, g, p["r_k"], p["ln_w"], p["ln_b"])


def _post_kernel(x_ref, attn_ref, yr_ref, wout_ref, gain_ref, wq_ref, kv_ref, wo_ref, o_ref):
    d = x_ref.shape[-1]
    dh = d // N_XATTN_HEADS
    mixed = jnp.concatenate([attn_ref[...], yr_ref[...]], axis=-1)
    x = x_ref[...] + _dot(mixed, wout_ref[...])
    h = _rms_norm(x, gain_ref[...]).astype(BF16)
    q = (_dot(h, wq_ref[...]) * dh ** -0.5).astype(BF16)
    heads = []
    for hd in range(N_XATTN_HEADS):
        s = _dot_nt(q[:, hd * dh:(hd + 1) * dh], kv_ref[:, hd * dh:(hd + 1) * dh])
        e = jnp.exp(s - jnp.max(s, axis=-1, keepdims=True))
        o = _dot(e.astype(BF16), kv_ref[:, d + hd * dh:d + (hd + 1) * dh])
        heads.append((o / jnp.sum(e, axis=-1, keepdims=True)).astype(BF16))
    o_ref[...] = x + _dot(jnp.concatenate(heads, axis=-1), wo_ref[...])


def _post(x2d, attn2d, yr, kv, layer, seq_len, p):
    n_tok, d = x2d.shape
    tm = TM_POST
    tiles_per_seq = seq_len // tm
    mem_len = kv.shape[2]
    tok = lambda width: pl.BlockSpec((tm, width), lambda i: (i, 0))
    return pl.pallas_call(
        _post_kernel,
        grid=(n_tok // tm,),
        in_specs=[tok(d), tok(D_GROUP), tok(D_GROUP), _layer_spec(p["w_out"], layer),
                  _layer_spec(p["norm_xattn"], layer), _layer_spec(p["wq"], layer),
                  pl.BlockSpec((None, None, mem_len, 2 * d), lambda i: (layer, i // tiles_per_seq, 0, 0)),
                  _layer_spec(p["wo"], layer)],
        out_specs=tok(d),
        out_shape=jax.ShapeDtypeStruct((n_tok, d), F32),
        compiler_params=_params(1),
        name="post",
    )(x2d, attn2d, yr, p["w_out"], p["norm_xattn"], p["wq"], kv, p["wo"])


def _ffn_kernel(*refs, final):
    x_ref, gain_ref, wg_ref, wu_ref, wd_ref = refs[:5]
    o_ref = refs[-1]
    x = x_ref[...]
    h = _rms_norm(x, gain_ref[...]).astype(BF16)
    d_ff = wg_ref.shape[-1]
    step = d_ff // FFN_CHUNKS
    y = x
    for ic in range(FFN_CHUNKS):
        cols = slice(ic * step, (ic + 1) * step)
        gate = _dot(h, wg_ref[:, cols])
        up = _dot(h, wu_ref[:, cols])
        act = (gate * jax.nn.sigmoid(gate) * up).astype(BF16)
        y = y + _dot(act, wd_ref[cols, :])
    if final:
        y = _rms_norm(y, refs[5][...])
    o_ref[...] = y


def _ffn(x2d, layer, p, final_gain):
    n_tok, d = x2d.shape
    tm = TM_FFN
    final = final_gain is not None
    tok = pl.BlockSpec((tm, d), lambda i: (i, 0))
    ins = [x2d, p["norm_ffn"], p["w_gate"], p["w_up"], p["w_down"]]
    in_specs = [tok] + [_layer_spec(a, layer) for a in ins[1:]]
    if final:
        ins.append(final_gain)
        in_specs.append(_const_spec(final_gain.shape, (0, 0)))
    return pl.pallas_call(
        functools.partial(_ffn_kernel, final=final),
        grid=(n_tok // tm,),
        in_specs=in_specs,
        out_specs=tok,
        out_shape=jax.ShapeDtypeStruct((n_tok, d), F32),
        compiler_params=_params(1),
        name="ffn",
    )(*ins)


def kernel(x, mem, mem_norm, norm_mix, w_in, attn_out_norm, rwkv_mu_rkv, rwkv_mu_lora, rwkv_mu_v, rwkv_w0, rwkv_w1, rwkv_w2, rwkv_a0, rwkv_a1, rwkv_a2, rwkv_v0, rwkv_v1, rwkv_v2, rwkv_g1, rwkv_g2, rwkv_k_k, rwkv_k_a, rwkv_r_k, rwkv_ln_w, rwkv_ln_b, w_out, norm_xattn, xattn_wq, xattn_wkv, xattn_wo, norm_ffn, ffn_w_gate, ffn_w_up, ffn_w_down, norm_final):
    bsz, seq, d = x.shape
    n_layers = w_in.shape[0]
    assert d == 2 * D_GROUP and seq % max(TM_MIX, TM_POST, TM_FFN, TW_WKV) == 0
    row = lambda t: t.reshape(t.shape[0], 1, -1)
    p = {
        "norm_mix": row(norm_mix), "w_in": w_in.astype(BF16),
        "mu_rkv": rwkv_mu_rkv, "mu_lora": rwkv_mu_lora, "mu_v": row(rwkv_mu_v),
        "w0": row(rwkv_w0), "w1": rwkv_w1.astype(BF16), "w2": rwkv_w2.astype(BF16),
        "a0": row(rwkv_a0), "a1": rwkv_a1.astype(BF16), "a2": rwkv_a2.astype(BF16),
        "v0": row(rwkv_v0), "v1": rwkv_v1.astype(BF16), "v2": rwkv_v2.astype(BF16),
        "g1": rwkv_g1.astype(BF16), "g2": rwkv_g2.astype(BF16),
        "k_k": row(rwkv_k_k), "k_a": row(rwkv_k_a), "r_k": row(rwkv_r_k),
        "ln_w": row(rwkv_ln_w), "ln_b": row(rwkv_ln_b),
        "w_out": w_out.astype(BF16), "norm_xattn": row(norm_xattn),
        "wq": xattn_wq.astype(BF16), "wo": xattn_wo.astype(BF16),
        "norm_ffn": row(norm_ffn), "w_gate": ffn_w_gate.astype(BF16),
        "w_up": ffn_w_up.astype(BF16), "w_down": ffn_w_down.astype(BF16),
    }
    attn_gain = row(attn_out_norm)
    mem_len = mem.shape[1]
    kv = _mem_kv(mem.reshape(bsz * mem_len, d), mem_norm.reshape(1, d), xattn_wkv.astype(BF16))
    kv = kv.reshape(n_layers, bsz, mem_len, 2 * d)

    x2d = x.reshape(bsz * seq, d)
    v_first = None
    for layer in range(n_layers):
        q, k, v, r, lw, k_mod, vv, kku, a, g = _mix_in(x2d, layer, seq, p, v_first)
        if layer == 0:
            v_first = vv
        attn = _dilated_attention(q, k, v, attn_gain, layer, seq)
        yr = _wkv(r, lw, k_mod, vv, kku, a, g, layer, seq, p)
        x2d = _post(x2d, attn, yr, kv, layer, seq, p)
        x2d = _ffn(x2d, layer, p, norm_final.reshape(1, d) if layer == n_layers - 1 else None)
    return x2d.reshape(bsz, seq, d)
```

```python
import functools

import jax
import jax.numpy as jnp
from jax import lax
from jax.experimental import pallas as pl
from jax.experimental.pallas import tpu as pltpu

F32 = jnp.float32
BF16 = jnp.bfloat16

HEAD_DIM = 64
PAIR = 2 * HEAD_DIM
D_GROUP = 512
N_PAIRS = D_GROUP // PAIR
DILATED_PATTERNS = ((128, 1), (512, 4), (2048, 16))
ATTN_BLOCK = 128
ATTN_SPAN = ATTN_BLOCK * max(d for _, d in DILATED_PATTERNS)
ATTN_BLOCKS_PER_STEP = 4
ATTN_NORM_ROWS = 256
N_XATTN_HEADS = 4
NORM_EPS = 1e-6
GN_EPS = 64e-5
MASK_VALUE = -1e30
WKV_CHUNK = 64

V7X_VMEM_LIMIT_BYTES = 56 * 1024 * 1024
TM_MIX = 512
TM_POST = 1024
TM_FFN = 1024
TW_WKV = 256
FFN_CHUNKS = 2


def _dot(a, b):
    return lax.dot_general(a, b, (((1,), (0,)), ((), ())), preferred_element_type=F32)


def _dot_nt(a, b):
    return lax.dot_general(a, b, (((1,), (1,)), ((), ())), preferred_element_type=F32)


def _dot_tn(a, b):
    return lax.dot_general(a, b, (((0,), (0,)), ((), ())), preferred_element_type=F32)


def _rms_norm(x, gain):
    return x * lax.rsqrt(jnp.mean(x * x, axis=-1, keepdims=True) + NORM_EPS) * gain


def _softplus(z):
    return jnp.maximum(z, 0.0) + jnp.log1p(jnp.exp(-jnp.abs(z)))


def _params(n_grid_dims):
    return pltpu.CompilerParams(dimension_semantics=("arbitrary",) * n_grid_dims,
                                vmem_limit_bytes=V7X_VMEM_LIMIT_BYTES)


def _const_spec(shape, index):
    return pl.BlockSpec(shape, lambda *_: index, pipeline_mode=pl.Buffered(1))


def _layer_spec(arr, layer):
    return _const_spec((None,) + arr.shape[1:], (layer,) + (0,) * (arr.ndim - 1))


def _mem_kv_kernel(mem_ref, gain_ref, wkv_ref, o_ref):
    mem_n = _rms_norm(mem_ref[...], gain_ref[...]).astype(BF16)
    o_ref[...] = _dot(mem_n, wkv_ref[...]).astype(BF16)


def _mem_kv(mem2d, gain, wkv):
    n_layers, d, d2 = wkv.shape
    rows = mem2d.shape[0]
    return pl.pallas_call(
        _mem_kv_kernel,
        grid=(n_layers,),
        in_specs=[pl.BlockSpec((rows, d), lambda l: (0, 0)),
                  pl.BlockSpec((1, d), lambda l: (0, 0)),
                  pl.BlockSpec((None, d, d2), lambda l: (l, 0, 0))],
        out_specs=pl.BlockSpec((None, rows, d2), lambda l: (l, 0, 0)),
        out_shape=jax.ShapeDtypeStruct((n_layers, rows, d2), BF16),
        compiler_params=_params(1),
        name="mem_kv",
    )(mem2d, gain, wkv)


def _mix_in_kernel(*refs, has_vres, tm, tiles_per_seq):
    (x_ref, gain_ref, w_in_ref, mu_rkv_ref, mu_lora_ref, w0_ref, w1_ref, w2_ref, a0_ref, a1_ref,
     a2_ref, g1_ref, g2_ref, kk_ref, ka_ref) = refs[:15]
    rest = refs[15:]
    if has_vres:
        mu_v_ref, v0_ref, v1_ref, v2_ref, vfirst_ref = rest[:5]
        rest = rest[5:]
    (q_out, k_out, v_out, r_out, lw_out, km_out, vv_out, kku_out, a_out, g_out,
     carry_h, carry_p) = rest

    @pl.when(pl.program_id(0) % tiles_per_seq == 0)
    def _start_of_sequence():
        carry_h[...] = jnp.zeros_like(carry_h)
        carry_p[...] = jnp.zeros_like(carry_p)

    row = lax.broadcasted_iota(jnp.int32, (tm, 1), 0)

    def minus_previous_token(t, carry_ref):
        prev = jnp.where(row == 0, carry_ref[...], pltpu.roll(t, 1, 0))
        carry_ref[...] = t[tm - 1:, :]
        return prev - t

    h = _rms_norm(x_ref[...], gain_ref[...])
    dh = minus_previous_token(h, carry_h)
    mu_lora = mu_lora_ref[...]

    def lora_hidden(mu_row, w_first_ref):
        return _dot((h + dh * mu_row).astype(BF16), w_first_ref[...])

    hb = h.astype(BF16)
    rkv = _dot(hb, w_in_ref[:, 3 * D_GROUP:])
    hid_w = lora_hidden(mu_lora[0:1], w1_ref)
    hid_a = lora_hidden(mu_lora[1:2], a1_ref)
    hid_g = lora_hidden(mu_lora[2:3], g1_ref)
    if has_vres:
        hid_v = lora_hidden(mu_v_ref[...], v1_ref)

    qkv = _dot(hb, w_in_ref[:, :3 * D_GROUP])

    def second(hidden, w_second_ref):
        return _dot(hidden.astype(BF16), w_second_ref[...])

    w_log = -_softplus(-(w0_ref[...] + second(jnp.tanh(hid_w), w2_ref))) - 0.5
    lw_out[...] = -jnp.exp(w_log)
    a = jax.nn.sigmoid(a0_ref[...] + second(hid_a, a2_ref))
    a_out[...] = a
    g_out[...] = second(jax.nn.sigmoid(hid_g), g2_ref)

    drkv = minus_previous_token(rkv, carry_p)
    mu_rkv = mu_rkv_ref[...]
    r_out[...] = rkv[:, 0:D_GROUP] + drkv[:, 0:D_GROUP] * mu_rkv[0:1]
    k = rkv[:, D_GROUP:2 * D_GROUP] + drkv[:, D_GROUP:2 * D_GROUP] * mu_rkv[1:2]
    kku_out[...] = k * kk_ref[...]
    km_out[...] = k * (1.0 + (a - 1.0) * ka_ref[...])
    v = rkv[:, 2 * D_GROUP:] + drkv[:, 2 * D_GROUP:] * mu_rkv[2:3]
    if has_vres:
        gate = jax.nn.sigmoid(v0_ref[...] + second(hid_v, v2_ref))
        v = v + (vfirst_ref[...] - v) * gate
    vv_out[...] = v

    q_out[...] = (qkv[:, 0:D_GROUP] * HEAD_DIM ** -0.5).astype(BF16)
    k_out[...] = qkv[:, D_GROUP:2 * D_GROUP].astype(BF16)
    v_out[...] = qkv[:, 2 * D_GROUP:3 * D_GROUP].astype(BF16)


def _mix_in(x2d, layer, seq_len, p, v_first):
    n_tok, d = x2d.shape
    tm = TM_MIX
    has_vres = layer > 0
    tok = lambda width: pl.BlockSpec((tm, width), lambda i: (i, 0))
    ins = [x2d, p["norm_mix"], p["w_in"], p["mu_rkv"], p["mu_lora"], p["w0"], p["w1"], p["w2"],
           p["a0"], p["a1"], p["a2"], p["g1"], p["g2"], p["k_k"], p["k_a"]]
    in_specs = [tok(d)] + [_layer_spec(a, layer) for a in ins[1:]]
    if has_vres:
        extra = [p["mu_v"], p["v0"], p["v1"], p["v2"]]
        ins += extra + [v_first]
        in_specs += [_layer_spec(a, layer - 1) for a in extra] + [tok(D_GROUP)]
    out_shape = ([jax.ShapeDtypeStruct((n_tok, D_GROUP), BF16)] * 3
                 + [jax.ShapeDtypeStruct((n_tok, D_GROUP), F32)] * 7)
    return pl.pallas_call(
        functools.partial(_mix_in_kernel, has_vres=has_vres, tm=tm, tiles_per_seq=seq_len // tm),
        grid=(n_tok // tm,),
        in_specs=in_specs,
        out_specs=[tok(D_GROUP)] * 10,
        out_shape=out_shape,
        scratch_shapes=[pltpu.VMEM((1, d), F32), pltpu.VMEM((1, 3 * D_GROUP), F32)],
        compiler_params=_params(1),
        name="mix_in",
    )(*ins)


def _attn_kernel(q_ref, k_ref, v_ref, gain_ref, o_ref, q_st, kv_st, acc_ref, stat_ref, prev_kv, *,
                 spans_per_seq, patterns):
    blk, span = ATTN_BLOCK, ATTN_SPAN
    j = pl.program_id(0) % spans_per_seq
    cur_base = (j % 2) * span
    prev_base = span - cur_base

    @pl.when(j == 0)
    def _no_previous_span():
        kv_st[:, pl.ds(prev_base, span), :] = jnp.zeros((2 * N_PAIRS, span, PAIR), F32)
        prev_kv[...] = jnp.zeros_like(prev_kv)

    for pair in range(N_PAIRS):
        cols = slice(pair * PAIR, (pair + 1) * PAIR)
        q_st[pair] = q_ref[:, cols].astype(F32)
        kv_st[pair, pl.ds(cur_base, span), :] = k_ref[:, cols].astype(F32)
        kv_st[N_PAIRS + pair, pl.ds(cur_base, span), :] = v_ref[:, cols].astype(F32)

    qi = lax.broadcasted_iota(jnp.int32, (2 * blk, 2 * blk), 0) % blk
    ki = lax.broadcasted_iota(jnp.int32, (2 * blk, 2 * blk), 1)
    dist = qi + blk - ki
    even = lax.broadcasted_iota(jnp.int32, (1, PAIR), 1) < HEAD_DIM
    ones = jnp.ones((2 * blk, PAIR), BF16)
    pairs = range(N_PAIRS)

    def run_pass(n_back, dil, first, last):
        blocks_per_sub = span // (blk * dil)
        shift = blocks_per_sub.bit_length() - 1
        stride = None if dil == 1 else dil
        previous_block_in_previous_span = blocks_per_sub == 1
        blocks_per_step = ATTN_BLOCKS_PER_STEP

        def rows(start):
            return pl.ds(start, blk, stride=stride) if stride else pl.ds(pl.multiple_of(start, blk), blk)

        def by_head(stacked):
            return jnp.where(even, stacked[:blk], stacked[blk:])

        def blocks_step(step, carry):
            starts, valids, items = [], [], []
            for ib in range(blocks_per_step):
                t = step * blocks_per_step + ib
                r = lax.shift_right_logical(t, shift)
                n = t & (blocks_per_sub - 1)
                start = n * (blk * dil) + r
                cur = cur_base + start
                prev = jnp.where(n > 0, cur - blk * dil, prev_base + span - blk * dil + r)
                first_key = jnp.where((n > 0) | (j > 0), 0, blk)
                starts.append(start)
                valids.append((dist >= 0) & (dist <= n_back) & (ki >= first_key))
                items += [(ib, p, cur, prev, r) for p in pairs]

            if not first:
                m_prev = [stat_ref[p, rows(starts[ib]), :] for ib, p, _, _, _ in items]
                l_prev = [stat_ref[N_PAIRS + p, rows(starts[ib]), :] for ib, p, _, _, _ in items]
                acc_prev = [acc_ref[p, rows(starts[ib]), :] for ib, p, _, _, _ in items]
            zero = jnp.zeros((blk, PAIR), BF16)
            work = [dict() for _ in items]

            def keys_or_values(i, which):
                ib, p, cur, prev, r = items[i]
                cols = slice(p * PAIR, (p + 1) * PAIR)
                slab = which * N_PAIRS + p
                cur_part = kv_st[slab, rows(cur), :].astype(BF16)
                if previous_block_in_previous_span:
                    prev_part = prev_kv[which, r, :, cols]
                    prev_kv[which, r, :, cols] = cur_part
                else:
                    prev_part = kv_st[slab, rows(prev), :].astype(BF16)
                return jnp.concatenate([prev_part, cur_part], axis=0)

            def scores(i):
                ib, p = items[i][:2]
                q = q_st[p, rows(starts[ib]), :].astype(BF16)
                qq = jnp.concatenate([jnp.where(even, q, zero), jnp.where(even, zero, q)], axis=0)
                work[i]["s"] = jnp.where(valids[ib], _dot_nt(qq, keys_or_values(i, 0)), MASK_VALUE)

            def weights(i):
                s = work[i].pop("s")
                m_new = jnp.broadcast_to(jnp.max(jnp.maximum(s[:, :blk], s[:, blk:]), axis=-1, keepdims=True),
                                         (2 * blk, PAIR))
                if not first:
                    swapped = pltpu.roll(m_prev[i], HEAD_DIM, 1)
                    m_old = jnp.concatenate([jnp.where(even, m_prev[i], swapped),
                                             jnp.where(even, swapped, m_prev[i])], axis=0)
                    m_new = jnp.maximum(m_new, m_old)
                    work[i]["alpha"] = by_head(jnp.exp(m_old - m_new))
                work[i]["m"] = by_head(m_new)
                work[i]["p"] = jnp.exp(s - jnp.concatenate([m_new, m_new], axis=1)).astype(BF16)

            def accumulate(i):
                ib, p = items[i][:2]
                pv = _dot(work[i].pop("p"), jnp.concatenate([keys_or_values(i, 1), ones], axis=1))
                num, l_new = by_head(pv[:, :PAIR]), by_head(pv[:, PAIR:])
                if not first:
                    alpha = work[i].pop("alpha")
                    l_new = alpha * l_prev[i] + l_new
                    num = alpha * acc_prev[i] + num
                if last:
                    acc_ref[p, rows(starts[ib]), :] = num / l_new
                else:
                    acc_ref[p, rows(starts[ib]), :] = num
                    stat_ref[p, rows(starts[ib]), :] = work[i].pop("m")
                    stat_ref[N_PAIRS + p, rows(starts[ib]), :] = l_new

            for stage in (scores, weights, accumulate):
                for i in range(len(items)):
                    stage(i)
            return carry

        lax.fori_loop(0, span // (blk * blocks_per_step), blocks_step, 0)

    for idx, (n_back, dil) in enumerate(patterns):
        run_pass(n_back, dil, first=idx == 0, last=idx == len(patterns) - 1)

    def normalise(ic, carry):
        rows = pl.ds(pl.multiple_of(ic * ATTN_NORM_ROWS, ATTN_NORM_ROWS), ATTN_NORM_ROWS)
        out = jnp.concatenate([acc_ref[p, rows, :] for p in pairs], axis=-1)
        o_ref[rows, :] = _rms_norm(out, gain_ref[...]).astype(BF16)
        return carry

    lax.fori_loop(0, span // ATTN_NORM_ROWS, normalise, 0)


def _dilated_attention(q, k, v, gain, layer, seq_len):
    n_tok, width = q.shape
    assert seq_len % ATTN_SPAN == 0, "sequence length must be a multiple of the largest dilated span"
    patterns = tuple(sorted(((w // d, d) for w, d in DILATED_PATTERNS), key=lambda nd: -nd[1]))
    assert patterns[-1][1] == 1 and all(ATTN_SPAN % (ATTN_BLOCK * d) == 0 for _, d in patterns)
    assert all(n_back <= ATTN_BLOCK for n_back, _ in patterns), \
        "a query block may only reach into the previous key block"
    tok = pl.BlockSpec((ATTN_SPAN, width), lambda i: (i, 0))
    tok_in = pl.BlockSpec((ATTN_SPAN, width), lambda i: (i, 0), pipeline_mode=pl.Buffered(1))
    return pl.pallas_call(
        functools.partial(_attn_kernel, spans_per_seq=seq_len // ATTN_SPAN, patterns=patterns),
        grid=(n_tok // ATTN_SPAN,),
        in_specs=[tok_in, tok_in, tok_in, _layer_spec(gain, layer)],
        out_specs=tok,
        out_shape=jax.ShapeDtypeStruct((n_tok, width), BF16),
        scratch_shapes=[pltpu.VMEM((N_PAIRS, ATTN_SPAN, PAIR), F32),
                        pltpu.VMEM((2 * N_PAIRS, 2 * ATTN_SPAN, PAIR), F32),
                        pltpu.VMEM((N_PAIRS, ATTN_SPAN, PAIR), F32),
                        pltpu.VMEM((2 * N_PAIRS, ATTN_SPAN, PAIR), F32),
                        pltpu.VMEM((2, ATTN_SPAN // ATTN_BLOCK, ATTN_BLOCK, width), BF16)],
        compiler_params=_params(1),
        name="attn",
    )(q, k, v, gain)


def _wkv_kernel(r_ref, lw_ref, k_ref, v_ref, kku_ref, a_ref, g_ref, rk_ref, lnw_ref, lnb_ref,
                o_ref, st_ref, *, tw, steps_per_seq):
    @pl.when(pl.program_id(0) % steps_per_seq == 0)
    def _start_of_sequence():
        st_ref[...] = jnp.zeros_like(st_ref)

    c = WKV_CHUNK
    even = lax.broadcasted_iota(jnp.int32, (1, PAIR), 1) < HEAD_DIM
    ri = lax.broadcasted_iota(jnp.int32, (PAIR, PAIR), 0)
    ci = lax.broadcasted_iota(jnp.int32, (PAIR, PAIR), 1)
    same_head = (ri // c) == (ci // c)
    strictly_lower = same_head & ((ri % c) > (ci % c))
    lower = same_head & ((ri % c) >= (ci % c))
    eye =(ri == ci).astype(F32)
    merge_masks = []
    s = 1
    while s < c:
        merge_masks.append(((ri // (2 * s)) == (ci // (2 * s)))
                           & ((ri // s) % 2 == 1) & ((ci // s) % 2 == 0))
        s *= 2

    def stack(t):
        return jnp.concatenate([jnp.where(even, t, 0.0), jnp.where(even, 0.0, t)], axis=0)

    def head_sum(t):
        s_even = jnp.sum(jnp.where(even, t, 0.0), axis=-1, keepdims=True)
        s_odd = jnp.sum(jnp.where(even, 0.0, t), axis=-1, keepdims=True)
        return jnp.where(even, s_even, s_odd)

    def bdot(a, b):
        return _dot(a.astype(BF16), b.astype(BF16))

    n_chunks = tw // c
    items = [(ic, pair) for ic in range(n_chunks) for pair in range(N_PAIRS)]

    def tile(ref_or_val, ic, pair):
        return ref_or_val[ic * c:(ic + 1) * c, pair * PAIR:(pair + 1) * PAIR]

    lw_all = lw_ref[...]
    row_in_chunk = lax.broadcasted_iota(jnp.int32, (tw, 1), 0) % c
    cum_all = lw_all
    shift = 1
    while shift < c:
        cum_all = cum_all + jnp.where(row_in_chunk >= shift, pltpu.roll(cum_all, shift, 0), 0.0)
        shift *= 2

    prep = []
    for ic, pair in items:
        lw, cum = tile(lw_all, ic, pair), tile(cum_all, ic, pair)
        total = cum[c - 1:, :]
        kku = tile(kku_ref, ic, pair)
        kk = kku / jnp.maximum(jnp.sqrt(head_sum(kku * kku)), 1e-12)
        b_vec = kk * tile(a_ref, ic, pair)
        k_mod = tile(k_ref, ic, pair)
        r = tile(r_ref, ic, pair)
        v = tile(v_ref, ic, pair)
        w_inv = jnp.exp(-cum)
        w_rest = jnp.exp(total - cum)
        r_t = r * jnp.exp(cum)
        la = stack(-kk * jnp.exp(cum - lw))
        v_swapped = pltpu.roll(v, HEAD_DIM, 1)
        prep.append(dict(
            r=r, v=v, k_mod=k_mod, r_t=r_t, la=la, decay=jnp.exp(total),
            lhs=jnp.concatenate([la, stack(r_t)], axis=0).astype(BF16),
            rhs=jnp.concatenate([stack(b_vec * w_inv), stack(k_mod * w_inv)], axis=0).astype(BF16),
            lv_swapped=jnp.concatenate([jnp.where(even, 0.0, v_swapped),
                                        jnp.where(even, v_swapped, 0.0)], axis=0).astype(BF16),
            bk_rest=jnp.concatenate([b_vec * w_rest, k_mod * w_rest], axis=0).astype(BF16)))

    aas = [_dot_nt(it["lhs"], it["rhs"]) for it in prep]
    a_abs = [jnp.where(strictly_lower, aa[:PAIR, :PAIR], 0.0) for aa in aas]
    xs = [it["la"] + _dot(jnp.where(strictly_lower, aa[:PAIR, PAIR:], 0.0).astype(BF16), it["lv_swapped"])
          for it, aa in zip(prep, aas)]
    m_mix = [jnp.concatenate([jnp.where(lower, aa[PAIR:, :PAIR], 0.0),
                              jnp.where(lower, aa[PAIR:, PAIR:], 0.0)], axis=1).astype(BF16) for aa in aas]
    t_invs = [eye + jnp.where(merge_masks[0], a_ab, 0.0) for a_ab in a_abs]
    for mask in merge_masks[1:]:
        t_bf = [t.astype(BF16) for t in t_invs]
        inner = [_dot(jnp.where(mask, a_ab, 0.0).astype(BF16), t) for a_ab, t in zip(a_abs, t_bf)]
        t_invs = [t + _dot(tb, w.astype(BF16)) for t, tb, w in zip(t_invs, t_bf, inner)]
    xs = [bdot(t, x) for t, x in zip(t_invs, xs)]

    out_rows = []
    states = [st_ref[pair] for pair in range(N_PAIRS)]
    for ic in range(n_chunks):
        chunk = [(prep[ic * N_PAIRS + pair], xs[ic * N_PAIRS + pair], m_mix[ic * N_PAIRS + pair])
                 for pair in range(N_PAIRS)]
        p_mats = [jnp.where(even, x[:c], x[c:]) for _, x, _ in chunk]
        q_mats = [pltpu.roll(jnp.where(even, x[c:], x[:c]), HEAD_DIM, 1) for _, x, _ in chunk]
        urs = [_dot_nt(jnp.concatenate([p_mat, it["r_t"]], axis=0).astype(BF16), state.astype(BF16))
               for (it, _, _), p_mat, state in zip(chunk, p_mats, states)]
        us = [ur[:c] + q_mat for ur, q_mat in zip(urs, q_mats)]
        y_mixes = [_dot(mm, jnp.concatenate([stack(u), stack(it["v"])], axis=0).astype(BF16))
                   for (it, _, mm), u in zip(chunk, us)]
        updates = [_dot_tn(jnp.concatenate([u, it["v"]], axis=0).astype(BF16), it["bk_rest"])
                   for (it, _, _), u in zip(chunk, us)]
        states = [state * it["decay"] + jnp.where(same_head, update, 0.0)
                  for (it, _, _), state, update in zip(chunk, states, updates)]
        outs = []
        for pair, ((it, _, _), ur, y_mix) in enumerate(zip(chunk, urs, y_mixes)):
            cols = slice(pair * PAIR, (pair + 1) * PAIR)
            y = ur[c:] + y_mix[:c] + y_mix[c:]
            mean = head_sum(y) * (1.0 / HEAD_DIM)
            yc = y - mean
            var = head_sum(yc * yc) * (1.0 / HEAD_DIM)
            y_norm = yc * lax.rsqrt(var + GN_EPS) * lnw_ref[:, cols] + lnb_ref[:, cols]
            bonus = head_sum(it["r"] * it["k_mod"] * rk_ref[:, cols]) * it["v"]
            outs.append(((y_norm + bonus) * tile(g_ref, ic, pair)).astype(BF16))
        out_rows.append(jnp.concatenate(outs, axis=-1))
    for pair in range(N_PAIRS):
        st_ref[pair] = states[pair]
    o_ref[...] = jnp.concatenate(out_rows, axis=0)


def _wkv(r, lw, k_mod, v, kku, a, g, layer, seq_len, p):
    n_tok = r.shape[0]
    tw = TW_WKV
    tok = pl.BlockSpec((tw, D_GROUP), lambda i: (i, 0))
    vec = lambda arr: _layer_spec(arr, layer)
    return pl.pallas_call(
        functools.partial(_wkv_kernel, tw=tw, steps_per_seq=seq_len // tw),
        grid=(n_tok // tw,),
        in_specs=[tok] * 7 + [vec(p["r_k"]), vec(p["ln_w"]), vec(p["ln_b"])],
        out_specs=tok,
        out_shape=jax.ShapeDtypeStruct((n_tok, D_GROUP), BF16),
        scratch_shapes=[pltpu.VMEM((N_PAIRS, PAIR, PAIR), F32)],
        compiler_params=_params(1),
        name="wkv",
    )(r, lw, k_mod, v, kku, a, g, p["r_k"], p["ln_w"], p["ln_b"])


def _post_kernel(x_ref, attn_ref, yr_ref, wout_ref, gain_ref, wq_ref, kv_ref, wo_ref, o_ref):
    d = x_ref.shape[-1]
    dh = d // N_XATTN_HEADS
    mixed = jnp.concatenate([attn_ref[...], yr_ref[...]], axis=-1)
    x = x_ref[...] + _dot(mixed, wout_ref[...])
    h = _rms_norm(x, gain_ref[...]).astype(BF16)
    q = (_dot(h, wq_ref[...]) * dh ** -0.5).astype(BF16)
    heads = []
    for hd in range(N_XATTN_HEADS):
        s = _dot_nt(q[:, hd * dh:(hd + 1) * dh], kv_ref[:, hd * dh:(hd + 1) * dh])
        e = jnp.exp(s - jnp.max(s, axis=-1, keepdims=True))
        o = _dot(e.astype(BF16), kv_ref[:, d + hd * dh:d + (hd + 1) * dh])
        heads.append((o / jnp.sum(e, axis=-1, keepdims=True)).astype(BF16))
    o_ref[...] = x + _dot(jnp.concatenate(heads, axis=-1), wo_ref[...])


def _post(x2d, attn2d, yr, kv, layer, seq_len, p):
    n_tok, d = x2d.shape
    tm = TM_POST
    tiles_per_seq = seq_len // tm
    mem_len = kv.shape[2]
    tok = lambda width: pl.BlockSpec((tm, width), lambda i: (i, 0))
    return pl.pallas_call(
        _post_kernel,
        grid=(n_tok // tm,),
        in_specs=[tok(d), tok(D_GROUP), tok(D_GROUP), _layer_spec(p["w_out"], layer),
                  _layer_spec(p["norm_xattn"], layer), _layer_spec(p["wq"], layer),
                  pl.BlockSpec((None, None, mem_len, 2 * d), lambda i: (layer, i // tiles_per_seq, 0, 0)),
                  _layer_spec(p["wo"], layer)],
        out_specs=tok(d),
        out_shape=jax.ShapeDtypeStruct((n_tok, d), F32),
        compiler_params=_params(1),
        name="post",
    )(x2d, attn2d, yr, p["w_out"], p["norm_xattn"], p["wq"], kv, p["wo"])


def _ffn_kernel(*refs, final):
    x_ref, gain_ref, wg_ref, wu_ref, wd_ref = refs[:5]
    o_ref = refs[-1]
    x = x_ref[...]
    h = _rms_norm(x, gain_ref[...]).astype(BF16)
    d_ff = wg_ref.shape[-1]
    step = d_ff // FFN_CHUNKS
    y = x
    for ic in range(FFN_CHUNKS):
        cols = slice(ic * step, (ic + 1) * step)
        gate = _dot(h, wg_ref[:, cols])
        up = _dot(h, wu_ref[:, cols])
        act = (gate * jax.nn.sigmoid(gate) * up).astype(BF16)
        y = y + _dot(act, wd_ref[cols, :])
    if final:
        y = _rms_norm(y, refs[5][...])
    o_ref[...] = y


def _ffn(x2d, layer, p, final_gain):
    n_tok, d = x2d.shape
    tm = TM_FFN
    final = final_gain is not None
    tok = pl.BlockSpec((tm, d), lambda i: (i, 0))
    ins = [x2d, p["norm_ffn"], p["w_gate"], p["w_up"], p["w_down"]]
    in_specs = [tok] + [_layer_spec(a, layer) for a in ins[1:]]
    if final:
        ins.append(final_gain)
        in_specs.append(_const_spec(final_gain.shape, (0, 0)))
    return pl.pallas_call(
        functools.partial(_ffn_kernel, final=final),
        grid=(n_tok // tm,),
        in_specs=in_specs,
        out_specs=tok,
        out_shape=jax.ShapeDtypeStruct((n_tok, d), F32),
        compiler_params=_params(1),
        name="ffn",
    )(*ins)


def kernel(x, mem, mem_norm, norm_mix, w_in, attn_out_norm, rwkv_mu_rkv, rwkv_mu_lora, rwkv_mu_v, rwkv_w0, rwkv_w1, rwkv_w2, rwkv_a0, rwkv_a1, rwkv_a2, rwkv_v0, rwkv_v1, rwkv_v2, rwkv_g1, rwkv_g2, rwkv_k_k, rwkv_k_a, rwkv_r_k, rwkv_ln_w, rwkv_ln_b, w_out, norm_xattn, xattn_wq, xattn_wkv, xattn_wo, norm_ffn, ffn_w_gate, ffn_w_up, ffn_w_down, norm_final):
    bsz, seq, d = x.shape
    n_layers = w_in.shape[0]
    assert d == 2 * D_GROUP and seq % max(TM_MIX, TM_POST, TM_FFN, TW_WKV) == 0
    row = lambda t: t.reshape(t.shape[0], 1, -1)
    p = {
        "norm_mix": row(norm_mix), "w_in": w_in.astype(BF16),
        "mu_rkv": rwkv_mu_rkv, "mu_lora": rwkv_mu_lora, "mu_v": row(rwkv_mu_v),
        "w0": row(rwkv_w0), "w1": rwkv_w1.astype(BF16), "w2": rwkv_w2.astype(BF16),
        "a0": row(rwkv_a0), "a1": rwkv_a1.astype(BF16), "a2": rwkv_a2.astype(BF16),
        "v0": row(rwkv_v0), "v1": rwkv_v1.astype(BF16), "v2": rwkv_v2.astype(BF16),
        "g1": rwkv_g1.astype(BF16), "g2": rwkv_g2.astype(BF16),
        "k_k": row(rwkv_k_k), "k_a": row(rwkv_k_a), "r_k": row(rwkv_r_k),
        "ln_w": row(rwkv_ln_w), "ln_b": row(rwkv_ln_b),
        "w_out": w_out.astype(BF16), "norm_xattn": row(norm_xattn),
        "wq": xattn_wq.astype(BF16), "wo": xattn_wo.astype(BF16),
        "norm_ffn": row(norm_ffn), "w_gate": ffn_w_gate.astype(BF16),
        "w_up": ffn_w_up.astype(BF16), "w_down": ffn_w_down.astype(BF16),
    }
    attn_gain = row(attn_out_norm)
    mem_len = mem.shape[1]
    kv = _mem_kv(mem.reshape(bsz * mem_len, d), mem_norm.reshape(1, d), xattn_wkv.astype(BF16))
    kv = kv.reshape(n_layers, bsz, mem_len, 2 * d)

    x2d = x.reshape(bsz * seq, d)
    v_first = None
    for layer in range(n_layers):
        q, k, v, r, lw, k_mod, vv, kku, a, g = _mix_in(x2d, layer, seq, p, v_first)
        if layer == 0:
            v_first = vv
        attn = _dilated_attention(q, k, v, attn_gain, layer, seq)
        yr = _wkv(r, lw, k_mod, vv, kku, a, g, layer, seq, p)
        x2d = _post(x2d, attn, yr, kv, layer, seq, p)
        x2d = _ffn(x2d, layer, p, norm_final.reshape(1, d) if layer == n_layers - 1 else None)
    return x2d.reshape(bsz, seq, d)
```
